```python
import jax
import jax.numpy as jnp
from jax import lax
import numpy as np

D_MODEL = 2048
BATCH = 2
SEQ = 8192
DEPTH = 2

CHUNK = 64
MIX_WIDTH = D_MODEL
RWKV_WIDTH = MIX_WIDTH // 2
POOL_WIDTH = MIX_WIDTH - RWKV_WIDTH
HEAD_SIZE = 64
N_HEADS = RWKV_WIDTH // HEAD_SIZE
POOL_WINDOWS = (2, 4, 8, 16)
N_POOL_GROUPS = len(POOL_WINDOWS)
POOL_GROUP = POOL_WIDTH // N_POOL_GROUPS
R_DECAY = max(32, int(round(1.8 * RWKV_WIDTH ** 0.5 / 32)) * 32)
R_AAA = max(32, int(round(1.8 * RWKV_WIDTH ** 0.5 / 32)) * 32)
R_MV = max(32, int(round(1.3 * RWKV_WIDTH ** 0.5 / 32)) * 32)
R_GATE = max(32, int(round(0.6 * RWKV_WIDTH ** 0.8 / 32)) * 32)
C_SHIFT = 3 * RWKV_WIDTH + R_DECAY + R_AAA + R_GATE
C_IN = C_SHIFT + POOL_WIDTH
D_FF = 4 * D_MODEL
ALPHA = (2.0 * DEPTH) ** 0.25
BETA = (8.0 * DEPTH) ** -0.25
LN_EPS = 1e-5
GN_EPS = 64e-5
L2_EPS = 1e-12

kernel_name = 'hybrid_rwkv7_multiscale_pool_deepnorm_encoder'


def _layer_norm(x, g, b):
    xf = x.astype(jnp.float32)
    mean = jnp.mean(xf, axis=-1, keepdims=True)
    var = jnp.mean(jnp.square(xf - mean), axis=-1, keepdims=True)
    y = (xf - mean) * lax.rsqrt(var + LN_EPS)
    return (y * g.astype(jnp.float32) + b.astype(jnp.float32)).astype(x.dtype)


def _token_shift(z):
    return jnp.pad(z, ((0, 0), (1, 0), (0, 0)))[:, :-1]


def _heads(t):
    bsz, s, _ = t.shape
    return t.reshape(bsz, s, N_HEADS, HEAD_SIZE)


def _rwkv7_recurrence(r, w, k, v, a, b):
    bsz, s, h, n = r.shape
    n_chunks = s // CHUNK

    def to_chunks(t):
        return t.transpose(1, 0, 2, 3).reshape(n_chunks, CHUNK, bsz, h, n)

    def step(state, inp):
        r_t, w_t, k_t, v_t, a_t, b_t = inp
        sa = jnp.einsum('bhvk,bhk->bhv', state, a_t)
        state = (state * w_t[:, :, None, :]
                 + sa[..., None] * b_t[:, :, None, :]
                 + v_t[..., None] * k_t[:, :, None, :])
        return state, jnp.einsum('bhvk,bhk->bhv', state, r_t)

    def chunk_step(state, chunk_inp):
        return lax.scan(step, state, chunk_inp)

    state0 = jnp.zeros((bsz, h, n, n), jnp.float32)
    xs = (to_chunks(r), to_chunks(w), to_chunks(k), to_chunks(v), to_chunks(a), to_chunks(b))
    _, y = lax.scan(chunk_step, state0, xs)
    return y.reshape(s, bsz, h, n).transpose(1, 0, 2, 3)


def _rwkv7_mix(z, v_first, mu, w0, w_up, a0, a_up, g_up, k_k, k_a, r_k, gn_g, gn_b, v_gate):
    zm = z + (_token_shift(z) - z) * mu
    o1, o2, o3 = RWKV_WIDTH, 2 * RWKV_WIDTH, 3 * RWKV_WIDTH
    o4 = o3 + R_DECAY
    o5 = o4 + R_AAA
    r, k, v = zm[..., :o1], zm[..., o1:o2], zm[..., o2:o3]
    xw, xa, xg = zm[..., o3:o4], zm[..., o4:o5], zm[..., o5:]
    w_pre = (w0 + jnp.tanh(xw) @ w_up).astype(jnp.float32)
    decay = jnp.exp(-jnp.exp(-jax.nn.softplus(-w_pre) - 0.5))
    a = jax.nn.sigmoid(a0 + xa @ a_up)
    g = jax.nn.sigmoid(xg) @ g_up
    if v_gate is None:
        v_first = v
    else:
        v0, v_down, v_up = v_gate
        v = v + (v_first - v) * jax.nn.sigmoid(v0 + (v @ v_down) @ v_up)
    kk = _heads((k * k_k).astype(jnp.float32))
    kk = kk / jnp.maximum(jnp.sqrt(jnp.sum(jnp.square(kk), axis=-1, keepdims=True)), L2_EPS)
    k = k * (1.0 + (a - 1.0) * k_a)
    rh = _heads(r.astype(jnp.float32))
    kh = _heads(k.astype(jnp.float32))
    vh = _heads(v.astype(jnp.float32))
    ah = _heads(a.astype(jnp.float32))
    y = _rwkv7_recurrence(rh, _heads(decay), kh, vh, -kk, kk * ah)
    mean = jnp.mean(y, axis=-1, keepdims=True)
    var = jnp.mean(jnp.square(y - mean), axis=-1, keepdims=True)
    y = (y - mean) * lax.rsqrt(var + GN_EPS)
    y = (y * gn_g.astype(jnp.float32).reshape(N_HEADS, HEAD_SIZE)
         + gn_b.astype(jnp.float32).reshape(N_HEADS, HEAD_SIZE))
    bonus = jnp.sum(rh * kh * r_k.astype(jnp.float32), axis=-1, keepdims=True) * vh
    out = (y + bonus).reshape(z.shape[0], z.shape[1], RWKV_WIDTH).astype(z.dtype) * g
    return out, v_first


def _pool_mix(u, pool_w, pool_scale):
    bsz, s, c = u.shape
    uf = u.astype(jnp.float32)
    cs = jnp.cumsum(uf, axis=1)
    pos = jnp.arange(1, s + 1, dtype=jnp.float32)
    outs = []
    for gi, win in enumerate(POOL_WINDOWS):
        sl = slice(gi * POOL_GROUP, (gi + 1) * POOL_GROUP)
        c_g = cs[:, :, sl]
        lag = jnp.pad(c_g, ((0, 0), (win, 0), (0, 0)))[:, :s]
        mean = (c_g - lag) / jnp.minimum(pos, float(win))[None, :, None]
        outs.append(mean - uf[:, :, sl])
    d = jnp.stack(outs, axis=2).astype(u.dtype)
    y = jnp.einsum('bsgc,gcd->bsgd', d, pool_w).reshape(bsz, s, c)
    return y * pool_scale


def setup_inputs(seed: int = 0) -> dict:
    key = jax.random.key(seed)
    ks = jax.random.split(key, 26)

    def nrm(k, shape, scale):
        return scale * jax.random.normal(k, shape, jnp.float32)

    L = DEPTH
    Lv = DEPTH - 1
    return {
        'x': jax.random.normal(ks[0], (BATCH, SEQ, D_MODEL), jnp.float32),
        'w_in': nrm(ks[1], (L, D_MODEL, C_IN), D_MODEL ** -0.5),
        'mu': jax.random.uniform(ks[2], (L, C_SHIFT), jnp.float32),
        'w0': jax.random.uniform(ks[3], (L, RWKV_WIDTH), jnp.float32, -6.0, 1.0),
        'w_up': nrm(ks[4], (L, R_DECAY, RWKV_WIDTH), 0.1 * R_DECAY ** -0.5),
        'a0': nrm(ks[5], (L, RWKV_WIDTH), 0.1),
        'a_up': nrm(ks[6], (L, R_AAA, RWKV_WIDTH), 0.1 * R_AAA ** -0.5),
        'g_up': nrm(ks[7], (L, R_GATE, RWKV_WIDTH), R_GATE ** -0.5),
        'v0': 1.0 + nrm(ks[8], (Lv, RWKV_WIDTH), 0.1),
        'v_down': nrm(ks[9], (Lv, RWKV_WIDTH, R_MV), RWKV_WIDTH ** -0.5),
        'v_up': nrm(ks[10], (Lv, R_MV, RWKV_WIDTH), 0.1 * R_MV ** -0.5),
        'k_k': 0.85 + nrm(ks[11], (L, RWKV_WIDTH), 0.05),
        'k_a': 1.0 + nrm(ks[12], (L, RWKV_WIDTH), 0.05),
        'r_k': nrm(ks[13], (L, N_HEADS, HEAD_SIZE), 0.1),
        'gn_g': 1.0 + nrm(ks[14], (L, RWKV_WIDTH), 0.02),
        'gn_b': nrm(ks[15], (L, RWKV_WIDTH), 0.02),
        'pool_w': nrm(ks[16], (L, N_POOL_GROUPS, POOL_GROUP, POOL_GROUP), POOL_GROUP ** -0.5),
        'pool_scale': 1.0 + nrm(ks[17], (L, POOL_WIDTH), 0.02),
        'w_out': nrm(ks[18], (L, MIX_WIDTH, D_MODEL), BETA * MIX_WIDTH ** -0.5),
        'ln1_g': 1.0 + nrm(ks[19], (L, D_MODEL), 0.02),
        'ln1_b': nrm(ks[20], (L, D_MODEL), 0.02),
        'mlp_w1': nrm(ks[21], (L, D_MODEL, D_FF), D_MODEL ** -0.5),
        'mlp_w2': nrm(ks[22], (L, D_FF, D_MODEL), BETA * D_FF ** -0.5),
        'ln2_g': 1.0 + nrm(ks[23], (L, D_MODEL), 0.02),
        'ln2_b': nrm(ks[24], (L, D_MODEL), 0.02),
    }


def reference(x, w_in, mu, w0, w_up, a0, a_up, g_up, v0, v_down, v_up, k_k, k_a, r_k,
              gn_g, gn_b, pool_w, pool_scale, w_out, ln1_g, ln1_b, mlp_w1, mlp_w2,
              ln2_g, ln2_b):
    v_first = None
    for l in range(DEPTH):
        z = x @ w_in[l]
        z_rwkv, z_pool = z[..., :C_SHIFT], z[..., C_SHIFT:]
        v_gate = None if l == 0 else (v0[l - 1], v_down[l - 1], v_up[l - 1])
        y_rwkv, v_first = _rwkv7_mix(z_rwkv, v_first, mu[l], w0[l], w_up[l], a0[l], a_up[l],
                                     g_up[l], k_k[l], k_a[l], r_k[l], gn_g[l], gn_b[l], v_gate)
        y_pool = _pool_mix(z_pool, pool_w[l], pool_scale[l])
        mix = jnp.concatenate([y_rwkv, y_pool], axis=-1) @ w_out[l]
        x = _layer_norm(ALPHA * x + mix, ln1_g[l], ln1_b[l])
        h = jnp.square(jax.nn.relu(x @ mlp_w1[l]))
        x = _layer_norm(ALPHA * x + h @ mlp_w2[l], ln2_g[l], ln2_b[l])
    return x
```

```python
import functools

import jax
import jax.numpy as jnp
from jax import lax
from jax.experimental import pallas as pl
from jax.experimental.pallas import tpu as pltpu

D_MODEL = 2048
BATCH = 2
SEQ = 8192
DEPTH = 2
RWKV_WIDTH = 1024
POOL_WIDTH = 1024
HEAD_SIZE = 64
POOL_WINDOWS = (2, 4, 8, 16)
POOL_GROUP = 256
R_DECAY = 64
R_AAA = 64
R_MV = 32
R_GATE = 160
C_SHIFT = 3 * RWKV_WIDTH + R_DECAY + R_AAA + R_GATE
C_IN = C_SHIFT + POOL_WIDTH
D_FF = 4 * D_MODEL
ALPHA = (2.0 * DEPTH) ** 0.25
LN_EPS = 1e-5
GN_EPS = 64e-5
L2_EPS = 1e-12

LANE = 128
CHUNK = 64
N_PAIRS = RWKV_WIDTH // LANE
MAX_WINDOW = max(POOL_WINDOWS)

C_LR = 3 * RWKV_WIDTH
C_LR_W = 3 * LANE
C_POOL = C_LR + C_LR_W
C_USED = C_POOL + POOL_WIDTH
C_PAD = 4608

VMEM_LIMIT = 52 * 1024 * 1024

F32 = jnp.float32
BF16 = jnp.bfloat16


def _bf(x):
    return x.astype(BF16)


def _dot(a, b):
    return jnp.dot(a, b, preferred_element_type=F32)


def _dot_nt(a, b):
    return lax.dot_general(a, b, (((1,), (1,)), ((), ())), preferred_element_type=F32)


def _dot_tn(a, b):
    return lax.dot_general(a, b, (((0,), (0,)), ((), ())), preferred_element_type=F32)


def _split_dot(x, w_bf, n_terms):
    acc = None
    rem = x
    for _ in range(n_terms):
        piece = _bf(rem)
        term = _dot(piece, w_bf)
        acc = term if acc is None else acc + term
        rem = rem - piece.astype(F32)
    return acc


def _sigmoid(x):
    return 1.0 / (1.0 + jnp.exp(-x))


def _layer_norm_rows(h, g, b):
    mean = jnp.mean(h, axis=-1, keepdims=True)
    d = h - mean
    var = jnp.mean(d * d, axis=-1, keepdims=True)
    return d * lax.rsqrt(var + LN_EPS) * g + b


def _matmul_kernel(x_ref, w_ref, o_ref):
    o_ref[...] = _dot(x_ref[...], w_ref[...])


def _in_proj(x_bf, w_bf):
    m, k = x_bf.shape
    n = w_bf.shape[1]
    bm, bn = 1024, 1536
    return pl.pallas_call(
        _matmul_kernel,
        grid=(m // bm, n // bn),
        in_specs=[pl.BlockSpec((bm, k), lambda i, j: (i, 0)),
                  pl.BlockSpec((k, bn), lambda i, j: (0, j))],
        out_specs=pl.BlockSpec((bm, bn), lambda i, j: (i, j)),
        out_shape=jax.ShapeDtypeStruct((m, n), F32),
        compiler_params=pltpu.CompilerParams(
            dimension_semantics=("arbitrary", "arbitrary"), vmem_limit_bytes=VMEM_LIMIT),
        name="in_proj",
    )(x_bf, w_bf)


PREP_ROWS = 256


def _prep_kernel(*refs, has_gate):
    it = iter(refs)
    z_ref, zh_ref = next(it), next(it)
    vf_ref = next(it) if has_gate else None
    mu_ref, w0_ref, wup_ref, a0_ref, aup_ref, gup_ref = (next(it) for _ in range(6))
    if has_gate:
        v0_ref, vdn_ref, vup_ref = next(it), next(it), next(it)
    kk_ref, ka_ref, pw_ref, ps_ref = (next(it) for _ in range(4))
    r_o, k_o, v_o, an_o, bb_o, cum_o, g_o, yp_o = (next(it) for _ in range(8))

    bm = PREP_ROWS
    i = pl.program_id(1)
    first = i == 0

    def section(lo, width):
        cur = z_ref[0, :, lo:lo + width]
        halo = jnp.where(first, 0.0, zh_ref[0, :, lo:lo + width])
        return cur, halo

    def token_shift_mix(lo, width):
        cur, halo = section(lo, width)
        row = lax.broadcasted_iota(jnp.int32, cur.shape, 0)
        prev = jnp.where(row == 0, halo[MAX_WINDOW - 1:MAX_WINDOW, :], pltpu.roll(cur, 1, axis=0))
        return cur + (prev - cur) * mu_ref[:, lo:lo + width]

    zl = token_shift_mix(C_LR, C_LR_W)
    xwa = zl[:, 0:LANE]
    xg = zl[:, LANE:C_LR_W]
    w_pre = w0_ref[...] + _dot(_bf(jnp.tanh(xwa)), wup_ref[...])
    lw = -jnp.exp(-0.5) * _sigmoid(w_pre)
    a = _sigmoid(a0_ref[...] + _dot(_bf(xwa), aup_ref[...]))
    g = _dot(_bf(_sigmoid(xg)), gup_ref[...])

    rr = lax.broadcasted_iota(jnp.int32, (bm, bm), 0)
    cc = lax.broadcasted_iota(jnp.int32, (bm, bm), 1)
    chunk_bits = CHUNK.bit_length() - 1
    same_chunk = jnp.right_shift(rr, chunk_bits) == jnp.right_shift(cc, chunk_bits)
    tri = jnp.where((rr >= cc) & same_chunk, 1.0, 0.0).astype(BF16)
    cum = _split_dot_left(tri, lw, 3)

    r = token_shift_mix(0, RWKV_WIDTH)
    k = token_shift_mix(RWKV_WIDTH, RWKV_WIDTH)
    v = token_shift_mix(2 * RWKV_WIDTH, RWKV_WIDTH)
    if has_gate:
        vf = jnp.concatenate([vf_ref[p] for p in range(N_PAIRS)], axis=1)
        low = _dot(_bf(v), vdn_ref[...])
        gate = _sigmoid(v0_ref[...] + _dot(_bf(low), vup_ref[...]))
        v = v + (vf - v) * gate

    kk = k * kk_ref[...]
    sr = lax.broadcasted_iota(jnp.int32, (2 * LANE, 2 * LANE), 0)
    sc = lax.broadcasted_iota(jnp.int32, (2 * LANE, 2 * LANE), 1)
    head_bits = HEAD_SIZE.bit_length() - 1
    seg = jnp.where(jnp.right_shift(sr, head_bits) == jnp.right_shift(sc, head_bits),
                    1.0, 0.0).astype(BF16)
    sq = kk * kk
    ss = jnp.concatenate(
        [_split_dot(sq[:, j:j + 2 * LANE], seg, 3) for j in range(0, RWKV_WIDTH, 2 * LANE)], axis=1)
    kkn = kk / jnp.maximum(jnp.sqrt(ss), L2_EPS)
    an_w = -kkn * jnp.exp(-lw)
    bb = kkn * a
    k_mod = k * (1.0 + (a - 1.0) * ka_ref[...])

    for p in range(N_PAIRS):
        sl = slice(p * LANE, (p + 1) * LANE)
        r_o[p] = r[:, sl]
        k_o[p] = k_mod[:, sl]
        v_o[p] = v[:, sl]
        an_o[p] = an_w[:, sl]
        bb_o[p] = bb[:, sl]
        cum_o[p] = cum[:, sl]
        g_o[p] = g[:, sl]

    pos = i * bm + lax.broadcasted_iota(jnp.int32, (bm, POOL_GROUP), 0) + 1
    outs = []
    for gi, win in enumerate(POOL_WINDOWS):
        lo = C_POOL + gi * POOL_GROUP
        u, halo = section(lo, POOL_GROUP)
        e = jnp.concatenate([halo, u], axis=0)
        step = 1
        while step < win:
            e = e + pltpu.roll(e, step, axis=0)
            step *= 2
        wsum = e[MAX_WINDOW:MAX_WINDOW + bm, :]
        cnt = jnp.minimum(pos, win).astype(F32)
        d = wsum / cnt - u
        outs.append(_dot(_bf(d), pw_ref[gi]))
    yp_o[0] = jnp.concatenate(outs, axis=1) * ps_ref[...]


def _split_dot_left(w_bf, x, n_terms):
    acc = None
    rem = x
    for _ in range(n_terms):
        piece = _bf(rem)
        term = _dot(w_bf, piece)
        acc = term if acc is None else acc + term
        rem = rem - piece.astype(F32)
    return acc


def _prep(z, vf, p, has_gate):
    bm = PREP_ROWS
    nb = SEQ // bm
    halo_blocks = bm // MAX_WINDOW

    def full(shape):
        return pl.BlockSpec(shape, lambda b, i: (0,) * len(shape))

    pair_spec = pl.BlockSpec((N_PAIRS, bm, LANE), lambda b, i: (b, i, 0))
    in_specs = [pl.BlockSpec((1, bm, C_USED), lambda b, i: (b, i, 0)),
                pl.BlockSpec((1, MAX_WINDOW, C_USED),
                             lambda b, i: (b, jnp.maximum(i * halo_blocks - 1, 0), 0))]
    args = [z, z]
    if has_gate:
        in_specs.append(pair_spec)
        args.append(vf)
    names = ["mu", "w0", "w_up", "a0", "a_up", "g_up"]
    if has_gate:
        names += ["v0", "v_down", "v_up"]
    names += ["k_k", "k_a", "pool_w", "pool_scale"]
    for nme in names:
        in_specs.append(full(p[nme].shape))
        args.append(p[nme])
    pair_shape = jax.ShapeDtypeStruct((BATCH * N_PAIRS, SEQ, LANE), F32)
    out_shape = [pair_shape] * 7 + [jax.ShapeDtypeStruct((BATCH, SEQ, POOL_WIDTH), F32)]
    out_specs = [pair_spec] * 7 + [pl.BlockSpec((1, bm, POOL_WIDTH), lambda b, i: (b, i, 0))]
    return pl.pallas_call(
        functools.partial(_prep_kernel, has_gate=has_gate),
        grid=(BATCH, nb),
        in_specs=in_specs,
        out_specs=out_specs,
        out_shape=out_shape,
        compiler_params=pltpu.CompilerParams(
            dimension_semantics=("arbitrary", "arbitrary"), vmem_limit_bytes=VMEM_LIMIT),
        name="prep_gate" if has_gate else "prep",
    )(*args)


REC_PAIRS = 4


def _rec_kernel(r_ref, k_ref, v_ref, an_ref, bb_ref, cum_ref, g_ref, gng_ref, gnb_ref, rk_ref,
                o_ref, s_ref):
    t = CHUNK
    n = 2 * t

    @pl.when(pl.program_id(1) == 0)
    def _():
        s_ref[...] = jnp.zeros_like(s_ref)

    lane = lax.broadcasted_iota(jnp.int32, (t, LANE), 1)
    head0 = lane < HEAD_SIZE
    row = lax.broadcasted_iota(jnp.int32, (n, n), 0)
    col = lax.broadcasted_iota(jnp.int32, (n, n), 1)
    strict = row > col
    incl = row >= col
    own = (row < t) == (col < HEAD_SIZE)
    eye = jnp.where(row == col, 1.0, 0.0)

    def stack(x):
        return jnp.concatenate([jnp.where(head0, x, 0.0), jnp.where(head0, 0.0, x)], axis=0)

    def unstack(x):
        return x[:t] + x[t:]

    for p in range(REC_PAIRS):
        cum = cum_ref[p]
        cum_end = cum[t - 1:t, :]
        decay_in = jnp.exp(cum)
        decay_out = jnp.exp(-cum)
        decay_rest = jnp.exp(cum_end - cum)
        r, k, v = r_ref[p], k_ref[p], v_ref[p]
        an, bb = an_ref[p], bb_ref[p]

        v_st = stack(v)
        v_m = _bf(v_st)
        ar = jnp.concatenate([_bf(stack(an * decay_in)), _bf(stack(r * decay_in))], axis=0)
        bk = jnp.concatenate([_bf(stack(bb * decay_out)), _bf(stack(k * decay_out))], axis=0)
        bk_end = jnp.concatenate([_bf(stack(bb * decay_rest)), _bf(stack(k * decay_rest))], axis=0)

        scores = _dot_nt(ar, bk)
        l_ab = jnp.where(strict, scores[:n, :n], 0.0)
        l_ak = jnp.where(strict, scores[:n, n:], 0.0)
        m_rb = jnp.where(incl, scores[n:, :n], 0.0)
        m_rk = jnp.where(incl, scores[n:, n:], 0.0)

        state = s_ref[p]
        from_state = _dot_nt(ar, _bf(state))
        rhs = from_state[:n] + _dot(_bf(l_ak), v_m)

        power = _bf(l_ab)
        inv = eye + l_ab
        for _ in range(5):
            power = _bf(_dot(power, power))
            inv = inv + _dot(_bf(inv), power)
        u = _dot(_bf(inv), _bf(rhs))
        uv = jnp.concatenate([_bf(u), v_m], axis=0)

        y_st = from_state[n:] + _dot(jnp.concatenate([_bf(m_rb), _bf(m_rk)], axis=1), uv)
        s_ref[p] = state * jnp.exp(cum_end) + _dot_tn(uv, bk_end)

        mean = jnp.sum(y_st, axis=-1, keepdims=True) * (1.0 / HEAD_SIZE)
        d = jnp.where(own, y_st - mean, 0.0)
        var = jnp.sum(d * d, axis=-1, keepdims=True) * (1.0 / HEAD_SIZE)
        yn = d * lax.rsqrt(var + GN_EPS)
        bonus = jnp.sum(stack(r * k * rk_ref[p]), axis=-1, keepdims=True) * v_st
        out = unstack(yn) * gng_ref[p] + gnb_ref[p] + unstack(bonus)
        o_ref[p] = out * g_ref[p]


def _recurrence(r, k, v, an, bb, cum, g, gn_g, gn_b, r_k):
    gp = REC_PAIRS
    n_groups = BATCH * N_PAIRS // gp
    seq_spec = pl.BlockSpec((gp, CHUNK, LANE), lambda q, c: (q, c, 0))
    par_spec = pl.BlockSpec((gp, 1, LANE), lambda q, c: (q % (N_PAIRS // gp), 0, 0))
    return pl.pallas_call(
        _rec_kernel,
        grid=(n_groups, SEQ // CHUNK),
        in_specs=[seq_spec] * 7 + [par_spec] * 3,
        out_specs=seq_spec,
        out_shape=jax.ShapeDtypeStruct((BATCH * N_PAIRS, SEQ, LANE), F32),
        scratch_shapes=[pltpu.VMEM((gp, 2 * CHUNK, LANE), F32)],
        compiler_params=pltpu.CompilerParams(
            dimension_semantics=("arbitrary", "arbitrary"), vmem_limit_bytes=VMEM_LIMIT),
        name="recurrence",
    )(r, k, v, an, bb, cum, g, gn_g, gn_b, r_k)


OUT_ROWS = 512


def _outproj_kernel(yr_ref, yp_ref, x_ref, w_ref, g_ref, b_ref, xo_ref, xb_ref):
    yr = _bf(jnp.concatenate([yr_ref[p] for p in range(N_PAIRS)], axis=1))
    yp = _bf(yp_ref[0])
    mix = _dot(yr, w_ref[0:RWKV_WIDTH, :]) + _dot(yp, w_ref[RWKV_WIDTH:, :])
    y = _layer_norm_rows(ALPHA * x_ref[0] + mix, g_ref[...], b_ref[...])
    xo_ref[0] = y
    xb_ref[0] = _bf(y)


def _out_proj(yr, yp, x, w_bf, g, b):
    bm = OUT_ROWS
    row_spec = pl.BlockSpec((1, bm, D_MODEL), lambda bi, i: (bi, i, 0))
    vec_spec = pl.BlockSpec((1, D_MODEL), lambda bi, i: (0, 0))
    return pl.pallas_call(
        _outproj_kernel,
        grid=(BATCH, SEQ // bm),
        in_specs=[pl.BlockSpec((N_PAIRS, bm, LANE), lambda bi, i: (bi, i, 0)),
                  pl.BlockSpec((1, bm, POOL_WIDTH), lambda bi, i: (bi, i, 0)),
                  row_spec,
                  pl.BlockSpec((D_MODEL, D_MODEL), lambda bi, i: (0, 0)),
                  vec_spec, vec_spec],
        out_specs=[row_spec, row_spec],
        out_shape=[jax.ShapeDtypeStruct((BATCH, SEQ, D_MODEL), F32),
                   jax.ShapeDtypeStruct((BATCH, SEQ, D_MODEL), BF16)],
        compiler_params=pltpu.CompilerParams(
            dimension_semantics=("arbitrary", "arbitrary"), vmem_limit_bytes=VMEM_LIMIT),
        name="out_proj_ln",
    )(yr, yp, x, w_bf, g, b)


MLP_ROWS = 512
MLP_FF = 1024


def _mlp_kernel(xb_ref, x_ref, w1_ref, w2_ref, g_ref, b_ref, xo_ref, xob_ref, acc_ref):
    f = pl.program_id(1)
    h = _dot(xb_ref[...], w1_ref[...])
    h = jnp.square(jnp.maximum(h, 0.0))
    part = _dot(_bf(h), w2_ref[...])

    @pl.when(f == 0)
    def _():
        acc_ref[...] = part

    @pl.when(f > 0)
    def _():
        acc_ref[...] += part

    @pl.when(f == pl.num_programs(1) - 1)
    def _():
        y = _layer_norm_rows(ALPHA * x_ref[...] + acc_ref[...], g_ref[...], b_ref[...])
        xo_ref[...] = y
        xob_ref[...] = _bf(y)


def _mlp(x_bf, x, w1_bf, w2_bf, g, b):
    m = x.shape[0]
    bm, bf = MLP_ROWS, MLP_FF
    row_spec = pl.BlockSpec((bm, D_MODEL), lambda i, f: (i, 0))
    vec_spec = pl.BlockSpec((1, D_MODEL), lambda i, f: (0, 0))
    return pl.pallas_call(
        _mlp_kernel,
        grid=(m // bm, D_FF // bf),
        in_specs=[row_spec, row_spec,
                  pl.BlockSpec((D_MODEL, bf), lambda i, f: (0, f)),
                  pl.BlockSpec((bf, D_MODEL), lambda i, f: (f, 0)),
                  vec_spec, vec_spec],
        out_specs=[row_spec, row_spec],
        out_shape=[jax.ShapeDtypeStruct((m, D_MODEL), F32),
                   jax.ShapeDtypeStruct((m, D_MODEL), BF16)],
        scratch_shapes=[pltpu.VMEM((bm, D_MODEL), F32)],
        compiler_params=pltpu.CompilerParams(
            dimension_semantics=("arbitrary", "arbitrary"), vmem_limit_bytes=VMEM_LIMIT),
        name="mlp_ln",
    )(x_bf, x, w1_bf, w2_bf, g, b)


def _pad_rows(w, rows, at=0):
    out = jnp.zeros((rows, w.shape[1]), w.dtype)
    return out.at[at:at + w.shape[0]].set(w)


def _layer_params(l, w_in, mu, w0, w_up, a0, a_up, g_up, v0, v_down, v_up, k_k, k_a, r_k,
                  gn_g, gn_b, pool_w, pool_scale):
    wi = w_in[l]
    zc = jnp.zeros((D_MODEL, C_POOL - C_SHIFT), F32)
    zt = jnp.zeros((D_MODEL, C_PAD - C_USED), F32)
    w_in_p = jnp.concatenate([wi[:, :C_SHIFT], zc, wi[:, C_SHIFT:], zt], axis=1)
    mu_p = jnp.concatenate([mu[l], jnp.zeros((C_USED - C_SHIFT,), F32)])[None, :]
    p = {
        "w_in": _bf(w_in_p),
        "mu": mu_p,
        "w0": w0[l][None, :],
        "w_up": _bf(_pad_rows(w_up[l], LANE, 0)),
        "a0": a0[l][None, :],
        "a_up": _bf(_pad_rows(a_up[l], LANE, R_DECAY)),
        "g_up": _bf(_pad_rows(g_up[l], 2 * LANE, 0)),
        "k_k": k_k[l][None, :],
        "k_a": k_a[l][None, :],
        "pool_w": _bf(pool_w[l]),
        "pool_scale": pool_scale[l][None, :],
        "r_k": r_k[l].reshape(N_PAIRS, 1, LANE),
        "gn_g": gn_g[l].reshape(N_PAIRS, 1, LANE),
        "gn_b": gn_b[l].reshape(N_PAIRS, 1, LANE),
    }
    if l > 0:
        p["v0"] = v0[l - 1][None, :]
        p["v_down"] = _bf(jnp.pad(v_down[l - 1], ((0, 0), (0, LANE - R_MV))))
        p["v_up"] = _bf(_pad_rows(v_up[l - 1], LANE, 0))
    return p


def kernel(x, w_in, mu, w0, w_up, a0, a_up, g_up, v0, v_down, v_up, k_k, k_a, r_k, gn_g, gn_b,
           pool_w, pool_scale, w_out, ln1_g, ln1_b, mlp_w1, mlp_w2, ln2_g, ln2_b):
    m = BATCH * SEQ
    x_bf = _bf(x)
    v_first = None
    for l in range(DEPTH):
        p = _layer_params(l, w_in, mu, w0, w_up, a0, a_up, g_up, v0, v_down, v_up, k_k, k_a,
                          r_k, gn_g, gn_b, pool_w, pool_scale)
        z = _in_proj(x_bf.reshape(m, D_MODEL), p["w_in"]).reshape(BATCH, SEQ, C_PAD)
        r, k, v, an, bb, cum, g, y_pool = _prep(z, v_first, p, has_gate=l > 0)
        if l == 0:
            v_first = v
        y_rwkv = _recurrence(r, k, v, an, bb, cum, g, p["gn_g"], p["gn_b"], p["r_k"])
        x, x_bf = _out_proj(y_rwkv, y_pool, x, _bf(w_out[l]), ln1_g[l][None, :], ln1_b[l][None, :])
        x2, x2_bf = _mlp(x_bf.reshape(m, D_MODEL), x.reshape(m, D_MODEL), _bf(mlp_w1[l]),
                         _bf(mlp_w2[l]), ln2_g[l][None, :], ln2_b[l][None, :])
        x, x_bf = x2.reshape(BATCH, SEQ, D_MODEL), x2_bf.reshape(BATCH, SEQ, D_MODEL)
    return x
```

```python
import functools

import jax
import jax.numpy as jnp
from jax import lax
from jax.experimental import pallas as pl
from jax.experimental.pallas import tpu as pltpu

D_MODEL = 2048
BATCH = 2
SEQ = 8192
DEPTH = 2
RWKV_WIDTH = 1024
POOL_WIDTH = 1024
HEAD_SIZE = 64
POOL_WINDOWS = (2, 4, 8, 16)
POOL_GROUP = 256
R_DECAY = 64
R_AAA = 64
R_MV = 32
R_GATE = 160
C_SHIFT = 3 * RWKV_WIDTH + R_DECAY + R_AAA + R_GATE
C_IN = C_SHIFT + POOL_WIDTH
D_FF = 4 * D_MODEL
ALPHA = (2.0 * DEPTH) ** 0.25
LN_EPS = 1e-5
GN_EPS = 64e-5
L2_EPS = 1e-12

LANE = 128
CHUNK = 64
N_PAIRS = RWKV_WIDTH // LANE
MAX_WINDOW = max(POOL_WINDOWS)

C_LR = 3 * RWKV_WIDTH
C_LR_W = 3 * LANE
C_POOL = C_LR + C_LR_W
C_USED = C_POOL + POOL_WIDTH
C_PAD = 4608

VMEM_LIMIT = 52 * 1024 * 1024

F32 = jnp.float32
BF16 = jnp.bfloat16


def _bf(x):
    return x.astype(BF16)


def _dot(a, b):
    return jnp.dot(a, b, preferred_element_type=F32)


def _dot_nt(a, b):
    return lax.dot_general(a, b, (((1,), (1,)), ((), ())), preferred_element_type=F32)


def _dot_tn(a, b):
    return lax.dot_general(a, b, (((0,), (0,)), ((), ())), preferred_element_type=F32)


def _split_dot(x, w_bf, n_terms):
    acc = None
    rem = x
    for _ in range(n_terms):
        piece = _bf(rem)
        term = _dot(piece, w_bf)
        acc = term if acc is None else acc + term
        rem = rem - piece.astype(F32)
    return acc


def _sigmoid(x):
    return 1.0 / (1.0 + jnp.exp(-x))


def _layer_norm_rows(h, g, b):
    mean = jnp.mean(h, axis=-1, keepdims=True)
    d = h - mean
    var = jnp.mean(d * d, axis=-1, keepdims=True)
    return d * lax.rsqrt(var + LN_EPS) * g + b


def _matmul_kernel(x_ref, w_ref, o_ref):
    o_ref[...] = _dot(x_ref[...], w_ref[...])


def _in_proj(x_bf, w_bf):
    m, k = x_bf.shape
    n = w_bf.shape[1]
    bm, bn = 1024, 1536
    return pl.pallas_call(
        _matmul_kernel,
        grid=(m // bm, n // bn),
        in_specs=[pl.BlockSpec((bm, k), lambda i, j: (i, 0)),
                  pl.BlockSpec((k, bn), lambda i, j: (0, j))],
        out_specs=pl.BlockSpec((bm, bn), lambda i, j: (i, j)),
        out_shape=jax.ShapeDtypeStruct((m, n), F32),
        compiler_params=pltpu.CompilerParams(
            dimension_semantics=("arbitrary", "arbitrary"), vmem_limit_bytes=VMEM_LIMIT),
        name="in_proj",
    )(x_bf, w_bf)


PREP_ROWS = 256


def _prep_kernel(*refs, has_gate):
    it = iter(refs)
    z_ref, zh_ref = next(it), next(it)
    vf_ref = next(it) if has_gate else None
    mu_ref, w0_ref, wup_ref, a0_ref, aup_ref, gup_ref = (next(it) for _ in range(6))
    if has_gate:
        v0_ref, vdn_ref, vup_ref = next(it), next(it), next(it)
    kk_ref, ka_ref, pw_ref, ps_ref = (next(it) for _ in range(4))
    r_o, k_o, v_o, an_o, bb_o, cum_o, g_o, yp_o = (next(it) for _ in range(8))

    bm = PREP_ROWS
    i = pl.program_id(1)
    first = i == 0

    def section(lo, width):
        cur = z_ref[0, :, lo:lo + width]
        halo = jnp.where(first, 0.0, zh_ref[0, :, lo:lo + width])
        return cur, halo

    def token_shift_mix(lo, width):
        cur, halo = section(lo, width)
        row = lax.broadcasted_iota(jnp.int32, cur.shape, 0)
        prev = jnp.where(row == 0, halo[MAX_WINDOW - 1:MAX_WINDOW, :], pltpu.roll(cur, 1, axis=0))
        return cur + (prev - cur) * mu_ref[:, lo:lo + width]

    zl = token_shift_mix(C_LR, C_LR_W)
    xwa = zl[:, 0:LANE]
    xg = zl[:, LANE:C_LR_W]
    w_pre = w0_ref[...] + _dot(_bf(jnp.tanh(xwa)), wup_ref[...])
    lw = -jnp.exp(-0.5) * _sigmoid(w_pre)
    a = _sigmoid(a0_ref[...] + _dot(_bf(xwa), aup_ref[...]))
    g = _dot(_bf(_sigmoid(xg)), gup_ref[...])

    rr = lax.broadcasted_iota(jnp.int32, (bm, bm), 0)
    cc = lax.broadcasted_iota(jnp.int32, (bm, bm), 1)
    chunk_bits = CHUNK.bit_length() - 1
    same_chunk = jnp.right_shift(rr, chunk_bits) == jnp.right_shift(cc, chunk_bits)
    tri = jnp.where((rr >= cc) & same_chunk, 1.0, 0.0).astype(BF16)
    cum = _split_dot_left(tri, lw, 3)

    r = token_shift_mix(0, RWKV_WIDTH)
    k = token_shift_mix(RWKV_WIDTH, RWKV_WIDTH)
    v = token_shift_mix(2 * RWKV_WIDTH, RWKV_WIDTH)
    if has_gate:
        vf = jnp.concatenate([vf_ref[p] for p in range(N_PAIRS)], axis=1)
        low = _dot(_bf(v), vdn_ref[...])
        gate = _sigmoid(v0_ref[...] + _dot(_bf(low), vup_ref[...]))
        v = v + (vf - v) * gate

    kk = k * kk_ref[...]
    sr = lax.broadcasted_iota(jnp.int32, (2 * LANE, 2 * LANE), 0)
    sc = lax.broadcasted_iota(jnp.int32, (2 * LANE, 2 * LANE), 1)
    head_bits = HEAD_SIZE.bit_length() - 1
    seg = jnp.where(jnp.right_shift(sr, head_bits) == jnp.right_shift(sc, head_bits),
                    1.0, 0.0).astype(BF16)
    sq = kk * kk
    ss = jnp.concatenate(
        [_split_dot(sq[:, j:j + 2 * LANE], seg, 3) for j in range(0, RWKV_WIDTH, 2 * LANE)], axis=1)
    kkn = kk / jnp.maximum(jnp.sqrt(ss), L2_EPS)
    an_w = -kkn * jnp.exp(-lw)
    bb = kkn * a
    k_mod = k * (1.0 + (a - 1.0) * ka_ref[...])

    for p in range(N_PAIRS):
        sl = slice(p * LANE, (p + 1) * LANE)
        r_o[p] = r[:, sl]
        k_o[p] = k_mod[:, sl]
        v_o[p] = v[:, sl]
        an_o[p] = an_w[:, sl]
        bb_o[p] = bb[:, sl]
        cum_o[p] = cum[:, sl]
        g_o[p] = g[:, sl]

    pos = i * bm + lax.broadcasted_iota(jnp.int32, (bm, POOL_GROUP), 0) + 1
    outs = []
    for gi, win in enumerate(POOL_WINDOWS):
        lo = C_POOL + gi * POOL_GROUP
        u, halo = section(lo, POOL_GROUP)
        e = jnp.concatenate([halo, u], axis=0)
        step = 1
        while step < win:
            e = e + pltpu.roll(e, step, axis=0)
            step *= 2
        wsum = e[MAX_WINDOW:MAX_WINDOW + bm, :]
        cnt = jnp.minimum(pos, win).astype(F32)
        d = wsum / cnt - u
        outs.append(_dot(_bf(d), pw_ref[gi]))
    yp_o[0] = jnp.concatenate(outs, axis=1) * ps_ref[...]


def _split_dot_left(w_bf, x, n_terms):
    acc = None
    rem = x
    for _ in range(n_terms):
        piece = _bf(rem)
        term = _dot(w_bf, piece)
        acc = term if acc is None else acc + term
        rem = rem - piece.astype(F32)
    return acc


def _prep(z, vf, p, has_gate):
    bm = PREP_ROWS
    nb = SEQ // bm
    halo_blocks = bm // MAX_WINDOW

    def full(shape):
        return pl.BlockSpec(shape, lambda b, i: (0,) * len(shape))

    pair_spec = pl.BlockSpec((N_PAIRS, bm, LANE), lambda b, i: (b, i, 0))
    in_specs = [pl.BlockSpec((1, bm, C_USED), lambda b, i: (b, i, 0)),
                pl.BlockSpec((1, MAX_WINDOW, C_USED),
                             lambda b, i: (b, jnp.maximum(i * halo_blocks - 1, 0), 0))]
    args = [z, z]
    if has_gate:
        in_specs.append(pair_spec)
        args.append(vf)
    names = ["mu", "w0", "w_up", "a0", "a_up", "g_up"]
    if has_gate:
        names += ["v0", "v_down", "v_up"]
    names += ["k_k", "k_a", "pool_w", "pool_scale"]
    for nme in names:
        in_specs.append(full(p[nme].shape))
        args.append(p[nme])
    pair_shape = jax.ShapeDtypeStruct((BATCH * N_PAIRS, SEQ, LANE), F32)
    out_shape = [pair_shape] * 7 + [jax.ShapeDtypeStruct((BATCH, SEQ, POOL_WIDTH), F32)]
    out_specs = [pair_spec] * 7 + [pl.BlockSpec((1, bm, POOL_WIDTH), lambda b, i: (b, i, 0))]
    return pl.pallas_call(
        functools.partial(_prep_kernel, has_gate=has_gate),
        grid=(BATCH, nb),
        in_specs=in_specs,
        out_specs=out_specs,
        out_shape=out_shape,
        compiler_params=pltpu.CompilerParams(
            dimension_semantics=("arbitrary", "arbitrary"), vmem_limit_bytes=VMEM_LIMIT),
        name="prep_gate" if has_gate else "prep",
    )(*args)


REC_PAIRS = 16


def _rec_kernel(r_ref, k_ref, v_ref, an_ref, bb_ref, cum_ref, g_ref, gng_ref, gnb_ref, rk_ref,
                o_ref, s_ref):
    t = CHUNK
    n = 2 * t

    @pl.when(pl.program_id(1) == 0)
    def _():
        s_ref[...] = jnp.zeros_like(s_ref)

    lane = lax.broadcasted_iota(jnp.int32, (t, LANE), 1)
    head0 = lane < HEAD_SIZE
    row = lax.broadcasted_iota(jnp.int32, (n, n), 0)
    col = lax.broadcasted_iota(jnp.int32, (n, n), 1)
    strict = row > col
    incl = row >= col
    own = (row < t) == (col < HEAD_SIZE)
    eye = jnp.where(row == col, 1.0, 0.0)

    def stack(x):
        return jnp.concatenate([jnp.where(head0, x, 0.0), jnp.where(head0, 0.0, x)], axis=0)

    def unstack(x):
        return x[:t] + x[t:]

    pairs = range(REC_PAIRS)
    states = [s_ref[p] for p in pairs]
    cums = [cum_ref[p] for p in pairs]
    rs, ks, vs = [r_ref[p] for p in pairs], [k_ref[p] for p in pairs], [v_ref[p] for p in pairs]
    ans, bbs = [an_ref[p] for p in pairs], [bb_ref[p] for p in pairs]

    cum_ends = [c[t - 1:t, :] for c in cums]
    decay_in = [jnp.exp(c) for c in cums]
    decay_out = [jnp.exp(-c) for c in cums]
    decay_rest = [jnp.exp(e - c) for c, e in zip(cums, cum_ends)]
    v_st = [stack(v) for v in vs]
    v_m = [_bf(x) for x in v_st]
    ar = [jnp.concatenate([_bf(stack(a * d)), _bf(stack(r * d))], axis=0)
          for a, r, d in zip(ans, rs, decay_in)]
    bk = [jnp.concatenate([_bf(stack(b * d)), _bf(stack(k * d))], axis=0)
          for b, k, d in zip(bbs, ks, decay_out)]
    bk_end = [jnp.concatenate([_bf(stack(b * d)), _bf(stack(k * d))], axis=0)
              for b, k, d in zip(bbs, ks, decay_rest)]

    scores = [_dot_nt(x, y) for x, y in zip(ar, bk)]
    l_ab = [jnp.where(strict, sc[:n, :n], 0.0) for sc in scores]
    l_ak = [jnp.where(strict, sc[:n, n:], 0.0) for sc in scores]
    m_rbk = [jnp.concatenate([_bf(jnp.where(incl, sc[n:, :n], 0.0)),
                              _bf(jnp.where(incl, sc[n:, n:], 0.0))], axis=1) for sc in scores]
    from_state = [_dot_nt(x, _bf(st)) for x, st in zip(ar, states)]
    rhs = [fs[:n] + _dot(_bf(l), vm) for fs, l, vm in zip(from_state, l_ak, v_m)]

    power = [_bf(l) for l in l_ab]
    inv = [eye + l for l in l_ab]
    for _ in range(5):
        power = [_bf(_dot(x, x)) for x in power]
        inv = [x + _dot(_bf(x), pw) for x, pw in zip(inv, power)]
    u = [_dot(_bf(x), _bf(y)) for x, y in zip(inv, rhs)]
    uv = [jnp.concatenate([_bf(x), vm], axis=0) for x, vm in zip(u, v_m)]
    y_st = [fs[n:] + _dot(m, x) for fs, m, x in zip(from_state, m_rbk, uv)]
    new_states = [st * jnp.exp(e) + _dot_tn(x, be)
                  for st, e, x, be in zip(states, cum_ends, uv, bk_end)]

    outs = []
    for p in pairs:
        mean = jnp.sum(y_st[p], axis=-1, keepdims=True) * (1.0 / HEAD_SIZE)
        d = jnp.where(own, y_st[p] - mean, 0.0)
        var = jnp.sum(d * d, axis=-1, keepdims=True) * (1.0 / HEAD_SIZE)
        yn = d * lax.rsqrt(var + GN_EPS)
        bonus = jnp.sum(stack(rs[p] * ks[p] * rk_ref[p]), axis=-1, keepdims=True) * v_st[p]
        out = unstack(yn) * gng_ref[p] + gnb_ref[p] + unstack(bonus)
        outs.append(out * g_ref[p])

    for p in pairs:
        s_ref[p] = new_states[p]
        o_ref[p] = outs[p]


def _recurrence(r, k, v, an, bb, cum, g, gn_g, gn_b, r_k):
    gp = REC_PAIRS
    n_groups = BATCH * N_PAIRS // gp
    gn_g, gn_b, r_k = (jnp.tile(x, (BATCH, 1, 1)) for x in (gn_g, gn_b, r_k))
    seq_spec = pl.BlockSpec((gp, CHUNK, LANE), lambda q, c: (q, c, 0))
    par_spec = pl.BlockSpec((gp, 1, LANE), lambda q, c: (q, 0, 0))
    return pl.pallas_call(
        _rec_kernel,
        grid=(n_groups, SEQ // CHUNK),
        in_specs=[seq_spec] * 7 + [par_spec] * 3,
        out_specs=seq_spec,
        out_shape=jax.ShapeDtypeStruct((BATCH * N_PAIRS, SEQ, LANE), F32),
        scratch_shapes=[pltpu.VMEM((gp, 2 * CHUNK, LANE), F32)],
        compiler_params=pltpu.CompilerParams(
            dimension_semantics=("arbitrary", "arbitrary"), vmem_limit_bytes=VMEM_LIMIT),
        name="recurrence",
    )(r, k, v, an, bb, cum, g, gn_g, gn_b, r_k)


OUT_ROWS = 512


def _outproj_kernel(yr_ref, yp_ref, x_ref, w_ref, g_ref, b_ref, xo_ref, xb_ref):
    yr = _bf(jnp.concatenate([yr_ref[p] for p in range(N_PAIRS)], axis=1))
    yp = _bf(yp_ref[0])
    mix = _dot(yr, w_ref[0:RWKV_WIDTH, :]) + _dot(yp, w_ref[RWKV_WIDTH:, :])
    y = _layer_norm_rows(ALPHA * x_ref[0] + mix, g_ref[...], b_ref[...])
    xo_ref[0] = y
    xb_ref[0] = _bf(y)


def _out_proj(yr, yp, x, w_bf, g, b):
    bm = OUT_ROWS
    row_spec = pl.BlockSpec((1, bm, D_MODEL), lambda bi, i: (bi, i, 0))
    vec_spec = pl.BlockSpec((1, D_MODEL), lambda bi, i: (0, 0))
    return pl.pallas_call(
        _outproj_kernel,
        grid=(BATCH, SEQ // bm),
        in_specs=[pl.BlockSpec((N_PAIRS, bm, LANE), lambda bi, i: (bi, i, 0)),
                  pl.BlockSpec((1, bm, POOL_WIDTH), lambda bi, i: (bi, i, 0)),
                  row_spec,
                  pl.BlockSpec((D_MODEL, D_MODEL), lambda bi, i: (0, 0)),
                  vec_spec, vec_spec],
        out_specs=[row_spec, row_spec],
        out_shape=[jax.ShapeDtypeStruct((BATCH, SEQ, D_MODEL), F32),
                   jax.ShapeDtypeStruct((BATCH, SEQ, D_MODEL), BF16)],
        compiler_params=pltpu.CompilerParams(
            dimension_semantics=("arbitrary", "arbitrary"), vmem_limit_bytes=VMEM_LIMIT),
        name="out_proj_ln",
    )(yr, yp, x, w_bf, g, b)


MLP_ROWS = 512
MLP_FF = 1024


def _mlp_kernel(xb_ref, x_ref, w1_ref, w2_ref, g_ref, b_ref, xo_ref, xob_ref, acc_ref):
    f = pl.program_id(1)
    h = _dot(xb_ref[...], w1_ref[...])
    h = jnp.square(jnp.maximum(h, 0.0))
    part = _dot(_bf(h), w2_ref[...])

    @pl.when(f == 0)
    def _():
        acc_ref[...] = part

    @pl.when(f > 0)
    def _():
        acc_ref[...] += part

    @pl.when(f == pl.num_programs(1) - 1)
    def _():
        y = _layer_norm_rows(ALPHA * x_ref[...] + acc_ref[...], g_ref[...], b_ref[...])
        xo_ref[...] = y
        xob_ref[...] = _bf(y)


def _mlp(x_bf, x, w1_bf, w2_bf, g, b):
    m = x.shape[0]
    bm, bf = MLP_ROWS, MLP_FF
    row_spec = pl.BlockSpec((bm, D_MODEL), lambda i, f: (i, 0))
    vec_spec = pl.BlockSpec((1, D_MODEL), lambda i, f: (0, 0))
    return pl.pallas_call(
        _mlp_kernel,
        grid=(m // bm, D_FF // bf),
        in_specs=[row_spec, row_spec,
                  pl.BlockSpec((D_MODEL, bf), lambda i, f: (0, f)),
                  pl.BlockSpec((bf, D_MODEL), lambda i, f: (f, 0)),
                  vec_spec, vec_spec],
        out_specs=[row_spec, row_spec],
        out_shape=[jax.ShapeDtypeStruct((m, D_MODEL), F32),
                   jax.ShapeDtypeStruct((m, D_MODEL), BF16)],
        scratch_shapes=[pltpu.VMEM((bm, D_MODEL), F32)],
        compiler_params=pltpu.CompilerParams(
            dimension_semantics=("arbitrary", "arbitrary"), vmem_limit_bytes=VMEM_LIMIT),
        name="mlp_ln",
    )(x_bf, x, w1_bf, w2_bf, g, b)


def _pad_rows(w, rows, at=0):
    out = jnp.zeros((rows, w.shape[1]), w.dtype)
    return out.at[at:at + w.shape[0]].set(w)


def _layer_params(l, w_in, mu, w0, w_up, a0, a_up, g_up, v0, v_down, v_up, k_k, k_a, r_k,
                  gn_g, gn_b, pool_w, pool_scale):
    wi = w_in[l]
    zc = jnp.zeros((D_MODEL, C_POOL - C_SHIFT), F32)
    zt = jnp.zeros((D_MODEL, C_PAD - C_USED), F32)
    w_in_p = jnp.concatenate([wi[:, :C_SHIFT], zc, wi[:, C_SHIFT:], zt], axis=1)
    mu_p = jnp.concatenate([mu[l], jnp.zeros((C_USED - C_SHIFT,), F32)])[None, :]
    p = {
        "w_in": _bf(w_in_p),
        "mu": mu_p,
        "w0": w0[l][None, :],
        "w_up": _bf(_pad_rows(w_up[l], LANE, 0)),
        "a0": a0[l][None, :],
        "a_up": _bf(_pad_rows(a_up[l], LANE, R_DECAY)),
        "g_up": _bf(_pad_rows(g_up[l], 2 * LANE, 0)),
        "k_k": k_k[l][None, :],
        "k_a": k_a[l][None, :],
        "pool_w": _bf(pool_w[l]),
        "pool_scale": pool_scale[l][None, :],
        "r_k": r_k[l].reshape(N_PAIRS, 1, LANE),
        "gn_g": gn_g[l].reshape(N_PAIRS, 1, LANE),
        "gn_b": gn_b[l].reshape(N_PAIRS, 1, LANE),
    }
    if l > 0:
        p["v0"] = v0[l - 1][None, :]
        p["v_down"] = _bf(jnp.pad(v_down[l - 1], ((0, 0), (0, LANE - R_MV))))
        p["v_up"] = _bf(_pad_rows(v_up[l - 1], LANE, 0))
    return p


def kernel(x, w_in, mu, w0, w_up, a0, a_up, g_up, v0, v_down, v_up, k_k, k_a, r_k, gn_g, gn_b,
           pool_w, pool_scale, w_out, ln1_g, ln1_b, mlp_w1, mlp_w2, ln2_g, ln2_b):
    m = BATCH * SEQ
    x_bf = _bf(x)
    v_first = None
    for l in range(DEPTH):
        p = _layer_params(l, w_in, mu, w0, w_up, a0, a_up, g_up, v0, v_down, v_up, k_k, k_a,
                          r_k, gn_g, gn_b, pool_w, pool_scale)
        z = _in_proj(x_bf.reshape(m, D_MODEL), p["w_in"]).reshape(BATCH, SEQ, C_PAD)
        r, k, v, an, bb, cum, g, y_pool = _prep(z, v_first, p, has_gate=l > 0)
        if l == 0:
            v_first = v
        y_rwkv = _recurrence(r, k, v, an, bb, cum, g, p["gn_g"], p["gn_b"], p["r_k"])
        x, x_bf = _out_proj(y_rwkv, y_pool, x, _bf(w_out[l]), ln1_g[l][None, :], ln1_b[l][None, :])
        x2, x2_bf = _mlp(x_bf.reshape(m, D_MODEL), x.reshape(m, D_MODEL), _bf(mlp_w1[l]),
                         _bf(mlp_w2[l]), ln2_g[l][None, :], ln2_b[l][None, :])
        x, x_bf = x2.reshape(BATCH, SEQ, D_MODEL), x2_bf.reshape(BATCH, SEQ, D_MODEL)
    return x
```

```python
import functools

import jax
import jax.numpy as jnp
from jax import lax
from jax.experimental import pallas as pl
from jax.experimental.pallas import tpu as pltpu

D_MODEL = 2048
BATCH = 2
SEQ = 8192
DEPTH = 2
RWKV_WIDTH = 1024
POOL_WIDTH = 1024
HEAD_SIZE = 64
POOL_WINDOWS = (2, 4, 8, 16)
POOL_GROUP = 256
R_DECAY = 64
R_AAA = 64
R_MV = 32
R_GATE = 160
C_SHIFT = 3 * RWKV_WIDTH + R_DECAY + R_AAA + R_GATE
C_IN = C_SHIFT + POOL_WIDTH
D_FF = 4 * D_MODEL
ALPHA = (2.0 * DEPTH) ** 0.25
LN_EPS = 1e-5
GN_EPS = 64e-5
L2_EPS = 1e-12

LANE = 128
CHUNK = 64
N_PAIRS = RWKV_WIDTH // LANE
MAX_WINDOW = max(POOL_WINDOWS)

C_LR = 3 * RWKV_WIDTH
C_LR_W = 3 * LANE
C_POOL = C_LR + C_LR_W
C_USED = C_POOL + POOL_WIDTH
C_PAD = 4608

VMEM_LIMIT = 52 * 1024 * 1024

F32 = jnp.float32
BF16 = jnp.bfloat16


def _bf(x):
    return x.astype(BF16)


def _dot(a, b):
    return jnp.dot(a, b, preferred_element_type=F32)


def _dot_nt(a, b):
    return lax.dot_general(a, b, (((1,), (1,)), ((), ())), preferred_element_type=F32)


def _dot_tn(a, b):
    return lax.dot_general(a, b, (((0,), (0,)), ((), ())), preferred_element_type=F32)


def _split_dot(x, w_bf, n_terms):
    acc = None
    rem = x
    for _ in range(n_terms):
        piece = _bf(rem)
        term = _dot(piece, w_bf)
        acc = term if acc is None else acc + term
        rem = rem - piece.astype(F32)
    return acc


def _sigmoid(x):
    return 1.0 / (1.0 + jnp.exp(-x))


def _layer_norm_rows(h, g, b):
    mean = jnp.mean(h, axis=-1, keepdims=True)
    d = h - mean
    var = jnp.mean(d * d, axis=-1, keepdims=True)
    return d * lax.rsqrt(var + LN_EPS) * g + b


def _matmul_kernel(x_ref, w_ref, o_ref):
    o_ref[...] = _dot(_bf(x_ref[...]), w_ref[...])


def _in_proj(x, w_bf, layer):
    m, k = x.shape
    n = w_bf.shape[2]
    bm, bn = 1024, 1536
    return pl.pallas_call(
        _matmul_kernel,
        grid=(m // bm, n // bn),
        in_specs=[pl.BlockSpec((bm, k), lambda i, j: (i, 0)),
                  pl.BlockSpec((None, k, bn), lambda i, j: (layer, 0, j))],
        out_specs=pl.BlockSpec((bm, bn), lambda i, j: (i, j)),
        out_shape=jax.ShapeDtypeStruct((m, n), F32),
        compiler_params=pltpu.CompilerParams(
            dimension_semantics=("arbitrary", "arbitrary"), vmem_limit_bytes=VMEM_LIMIT),
        name="in_proj",
    )(x, w_bf)


PREP_ROWS = 256


def _prep_kernel(*refs, has_gate):
    it = iter(refs)
    z_ref, zh_ref = next(it), next(it)
    vf_ref = next(it) if has_gate else None
    mu_ref, w0_ref, wup_ref, a0_ref, aup_ref, gup_ref = (next(it) for _ in range(6))
    if has_gate:
        v0_ref, vdn_ref, vup_ref = next(it), next(it), next(it)
    kk_ref, ka_ref, pw_ref, ps_ref = (next(it) for _ in range(4))
    r_o, k_o, v_o, an_o, bb_o, cum_o, g_o, yp_o = (next(it) for _ in range(8))

    bm = PREP_ROWS
    i = pl.program_id(1)
    first = i == 0

    def section(lo, width):
        cur = z_ref[0, :, lo:lo + width]
        halo = jnp.where(first, 0.0, zh_ref[0, :, lo:lo + width])
        return cur, halo

    def token_shift_mix(lo, width):
        cur, halo = section(lo, width)
        row = lax.broadcasted_iota(jnp.int32, cur.shape, 0)
        prev = jnp.where(row == 0, halo[MAX_WINDOW - 1:MAX_WINDOW, :], pltpu.roll(cur, 1, axis=0))
        return cur + (prev - cur) * mu_ref[:, lo:lo + width]

    zl = token_shift_mix(C_LR, C_LR_W)
    xwa = zl[:, 0:LANE]
    xg = zl[:, LANE:C_LR_W]
    w_pre = w0_ref[...] + _dot(_bf(jnp.tanh(xwa)), wup_ref[...])
    lw = -jnp.exp(-0.5) * _sigmoid(w_pre)
    a = _sigmoid(a0_ref[...] + _dot(_bf(xwa), aup_ref[...]))
    g = _dot(_bf(_sigmoid(xg)), gup_ref[...])

    rr = lax.broadcasted_iota(jnp.int32, (bm, bm), 0)
    cc = lax.broadcasted_iota(jnp.int32, (bm, bm), 1)
    chunk_bits = CHUNK.bit_length() - 1
    same_chunk = jnp.right_shift(rr, chunk_bits) == jnp.right_shift(cc, chunk_bits)
    tri = jnp.where((rr >= cc) & same_chunk, 1.0, 0.0).astype(BF16)
    cum = _split_dot_left(tri, lw, 3)

    r = token_shift_mix(0, RWKV_WIDTH)
    k = token_shift_mix(RWKV_WIDTH, RWKV_WIDTH)
    v = token_shift_mix(2 * RWKV_WIDTH, RWKV_WIDTH)
    if has_gate:
        vf = jnp.concatenate([vf_ref[p] for p in range(N_PAIRS)], axis=1)
        low = _dot(_bf(v), vdn_ref[...])
        gate = _sigmoid(v0_ref[...] + _dot(_bf(low), vup_ref[...]))
        v = v + (vf - v) * gate

    kk = k * kk_ref[...]
    sr = lax.broadcasted_iota(jnp.int32, (2 * LANE, 2 * LANE), 0)
    sc = lax.broadcasted_iota(jnp.int32, (2 * LANE, 2 * LANE), 1)
    head_bits = HEAD_SIZE.bit_length() - 1
    seg = jnp.where(jnp.right_shift(sr, head_bits) == jnp.right_shift(sc, head_bits),
                    1.0, 0.0).astype(BF16)
    sq = kk * kk
    ss = jnp.concatenate(
        [_split_dot(sq[:, j:j + 2 * LANE], seg, 3) for j in range(0, RWKV_WIDTH, 2 * LANE)], axis=1)
    kkn = kk / jnp.maximum(jnp.sqrt(ss), L2_EPS)
    an_w = -kkn * jnp.exp(-lw)
    bb = kkn * a
    k_mod = k * (1.0 + (a - 1.0) * ka_ref[...])

    for p in range(N_PAIRS):
        sl = slice(p * LANE, (p + 1) * LANE)
        r_o[p] = r[:, sl]
        k_o[p] = k_mod[:, sl]
        v_o[p] = v[:, sl]
        an_o[p] = an_w[:, sl]
        bb_o[p] = bb[:, sl]
        cum_o[p] = cum[:, sl]
        g_o[p] = g[:, sl]

    pos = i * bm + lax.broadcasted_iota(jnp.int32, (bm, POOL_GROUP), 0) + 1
    outs = []
    for gi, win in enumerate(POOL_WINDOWS):
        lo = C_POOL + gi * POOL_GROUP
        u, halo = section(lo, POOL_GROUP)
        e = jnp.concatenate([halo, u], axis=0)
        step = 1
        while step < win:
            e = e + pltpu.roll(e, step, axis=0)
            step *= 2
        wsum = e[MAX_WINDOW:MAX_WINDOW + bm, :]
        cnt = jnp.minimum(pos, win).astype(F32)
        d = wsum / cnt - u
        outs.append(_dot(_bf(d), pw_ref[gi]))
    yp_o[0] = jnp.concatenate(outs, axis=1) * ps_ref[...]


def _split_dot_left(w_bf, x, n_terms):
    acc = None
    rem = x
    for _ in range(n_terms):
        piece = _bf(rem)
        term = _dot(w_bf, piece)
        acc = term if acc is None else acc + term
        rem = rem - piece.astype(F32)
    return acc


def _prep(z, vf, p, has_gate):
    bm = PREP_ROWS
    nb = SEQ // bm
    halo_blocks = bm // MAX_WINDOW

    def full(shape):
        return pl.BlockSpec(shape, lambda b, i: (0,) * len(shape))

    pair_spec = pl.BlockSpec((N_PAIRS, bm, LANE), lambda b, i: (b, i, 0))
    in_specs = [pl.BlockSpec((1, bm, C_USED), lambda b, i: (b, i, 0)),
                pl.BlockSpec((1, MAX_WINDOW, C_USED),
                             lambda b, i: (b, jnp.maximum(i * halo_blocks - 1, 0), 0))]
    args = [z, z]
    if has_gate:
        in_specs.append(pair_spec)
        args.append(vf)
    names = ["mu", "w0", "w_up", "a0", "a_up", "g_up"]
    if has_gate:
        names += ["v0", "v_down", "v_up"]
    names += ["k_k", "k_a", "pool_w", "pool_scale"]
    for nme in names:
        in_specs.append(full(p[nme].shape))
        args.append(p[nme])
    pair_shape = jax.ShapeDtypeStruct((BATCH * N_PAIRS, SEQ, LANE), F32)
    out_shape = [pair_shape] * 7 + [jax.ShapeDtypeStruct((BATCH, SEQ, POOL_WIDTH), F32)]
    out_specs = [pair_spec] * 7 + [pl.BlockSpec((1, bm, POOL_WIDTH), lambda b, i: (b, i, 0))]
    return pl.pallas_call(
        functools.partial(_prep_kernel, has_gate=has_gate),
        grid=(BATCH, nb),
        in_specs=in_specs,
        out_specs=out_specs,
        out_shape=out_shape,
        compiler_params=pltpu.CompilerParams(
            dimension_semantics=("arbitrary", "arbitrary"), vmem_limit_bytes=VMEM_LIMIT),
        name="prep_gate" if has_gate else "prep",
    )(*args)


REC_PAIRS = 16


def _rec_kernel(r_ref, k_ref, v_ref, an_ref, bb_ref, cum_ref, g_ref, gng_ref, gnb_ref, rk_ref,
                o_ref, s_ref):
    t = CHUNK
    n = 2 * t

    @pl.when(pl.program_id(1) == 0)
    def _():
        s_ref[...] = jnp.zeros_like(s_ref)

    lane = lax.broadcasted_iota(jnp.int32, (t, LANE), 1)
    head0 = lane < HEAD_SIZE
    row = lax.broadcasted_iota(jnp.int32, (n, n), 0)
    col = lax.broadcasted_iota(jnp.int32, (n, n), 1)
    strict = row > col
    incl = row >= col
    own = (row < t) == (col < HEAD_SIZE)
    eye = jnp.where(row == col, 1.0, 0.0)

    def stack(x):
        return jnp.concatenate([jnp.where(head0, x, 0.0), jnp.where(head0, 0.0, x)], axis=0)

    def unstack(x):
        return x[:t] + x[t:]

    pairs = range(REC_PAIRS)
    states = [s_ref[p] for p in pairs]
    cums = [cum_ref[p] for p in pairs]
    rs, ks, vs = [r_ref[p] for p in pairs], [k_ref[p] for p in pairs], [v_ref[p] for p in pairs]
    ans, bbs = [an_ref[p] for p in pairs], [bb_ref[p] for p in pairs]

    cum_ends = [c[t - 1:t, :] for c in cums]
    decay_in = [jnp.exp(c) for c in cums]
    decay_out = [jnp.exp(-c) for c in cums]
    decay_rest = [jnp.exp(e - c) for c, e in zip(cums, cum_ends)]
    v_st = [stack(v) for v in vs]
    v_m = [_bf(x) for x in v_st]
    ar = [jnp.concatenate([_bf(stack(a * d)), _bf(stack(r * d))], axis=0)
          for a, r, d in zip(ans, rs, decay_in)]
    bk = [jnp.concatenate([_bf(stack(b * d)), _bf(stack(k * d))], axis=0)
          for b, k, d in zip(bbs, ks, decay_out)]
    bk_end = [jnp.concatenate([_bf(stack(b * d)), _bf(stack(k * d))], axis=0)
              for b, k, d in zip(bbs, ks, decay_rest)]

    scores = [_dot_nt(x, y) for x, y in zip(ar, bk)]
    l_ab = [jnp.where(strict, sc[:n, :n], 0.0) for sc in scores]
    l_ak = [jnp.where(strict, sc[:n, n:], 0.0) for sc in scores]
    m_rbk = [jnp.concatenate([_bf(jnp.where(incl, sc[n:, :n], 0.0)),
                              _bf(jnp.where(incl, sc[n:, n:], 0.0))], axis=1) for sc in scores]
    from_state = [_dot_nt(x, _bf(st)) for x, st in zip(ar, states)]
    rhs = [fs[:n] + _dot(_bf(l), vm) for fs, l, vm in zip(from_state, l_ak, v_m)]

    power = [_bf(l) for l in l_ab]
    inv = [eye + l for l in l_ab]
    for _ in range(5):
        power = [_bf(_dot(x, x)) for x in power]
        inv = [x + _dot(_bf(x), pw) for x, pw in zip(inv, power)]
    u = [_dot(_bf(x), _bf(y)) for x, y in zip(inv, rhs)]
    uv = [jnp.concatenate([_bf(x), vm], axis=0) for x, vm in zip(u, v_m)]
    y_st = [fs[n:] + _dot(m, x) for fs, m, x in zip(from_state, m_rbk, uv)]
    new_states = [st * jnp.exp(e) + _dot_tn(x, be)
                  for st, e, x, be in zip(states, cum_ends, uv, bk_end)]

    outs = []
    for p in pairs:
        mean = jnp.sum(y_st[p], axis=-1, keepdims=True) * (1.0 / HEAD_SIZE)
        d = jnp.where(own, y_st[p] - mean, 0.0)
        var = jnp.sum(d * d, axis=-1, keepdims=True) * (1.0 / HEAD_SIZE)
        yn = d * lax.rsqrt(var + GN_EPS)
        bonus = jnp.sum(stack(rs[p] * ks[p] * rk_ref[p]), axis=-1, keepdims=True) * v_st[p]
        out = unstack(yn) * gng_ref[p] + gnb_ref[p] + unstack(bonus)
        outs.append(out * g_ref[p])

    for p in pairs:
        s_ref[p] = new_states[p]
        o_ref[p] = outs[p]


def _recurrence(r, k, v, an, bb, cum, g, gn_g, gn_b, r_k):
    gp = REC_PAIRS
    n_groups = BATCH * N_PAIRS // gp
    gn_g, gn_b, r_k = (jnp.tile(x, (BATCH, 1, 1)) for x in (gn_g, gn_b, r_k))
    seq_spec = pl.BlockSpec((gp, CHUNK, LANE), lambda q, c: (q, c, 0))
    par_spec = pl.BlockSpec((gp, 1, LANE), lambda q, c: (q, 0, 0))
    return pl.pallas_call(
        _rec_kernel,
        grid=(n_groups, SEQ // CHUNK),
        in_specs=[seq_spec] * 7 + [par_spec] * 3,
        out_specs=seq_spec,
        out_shape=jax.ShapeDtypeStruct((BATCH * N_PAIRS, SEQ, LANE), F32),
        scratch_shapes=[pltpu.VMEM((gp, 2 * CHUNK, LANE), F32)],
        compiler_params=pltpu.CompilerParams(
            dimension_semantics=("arbitrary", "arbitrary"), vmem_limit_bytes=VMEM_LIMIT),
        name="recurrence",
    )(r, k, v, an, bb, cum, g, gn_g, gn_b, r_k)


OUT_ROWS = 512


def _outproj_kernel(yr_ref, yp_ref, x_ref, w_ref, g_ref, b_ref, xo_ref, xb_ref):
    yr = _bf(jnp.concatenate([yr_ref[p] for p in range(N_PAIRS)], axis=1))
    yp = _bf(yp_ref[0])
    mix = _dot(yr, w_ref[0:RWKV_WIDTH, :]) + _dot(yp, w_ref[RWKV_WIDTH:, :])
    y = _layer_norm_rows(ALPHA * x_ref[0] + mix, g_ref[...], b_ref[...])
    xo_ref[0] = y
    xb_ref[0] = _bf(y)


def _out_proj(yr, yp, x, w_bf, g, b, layer):
    bm = OUT_ROWS
    row_spec = pl.BlockSpec((1, bm, D_MODEL), lambda bi, i: (bi, i, 0))
    vec_spec = pl.BlockSpec((None, 1, D_MODEL), lambda bi, i: (layer, 0, 0))
    return pl.pallas_call(
        _outproj_kernel,
        grid=(BATCH, SEQ // bm),
        in_specs=[pl.BlockSpec((N_PAIRS, bm, LANE), lambda bi, i: (bi, i, 0)),
                  pl.BlockSpec((1, bm, POOL_WIDTH), lambda bi, i: (bi, i, 0)),
                  row_spec,
                  pl.BlockSpec((None, D_MODEL, D_MODEL), lambda bi, i: (layer, 0, 0)),
                  vec_spec, vec_spec],
        out_specs=[row_spec, row_spec],
        out_shape=[jax.ShapeDtypeStruct((BATCH, SEQ, D_MODEL), F32),
                   jax.ShapeDtypeStruct((BATCH, SEQ, D_MODEL), BF16)],
        compiler_params=pltpu.CompilerParams(
            dimension_semantics=("arbitrary", "arbitrary"), vmem_limit_bytes=VMEM_LIMIT),
        name="out_proj_ln",
    )(yr, yp, x, w_bf, g, b)


MLP_ROWS = 1024
MLP_FF = 512
MLP_STEPS = D_FF // MLP_FF
MLP_RES_ROWS = MLP_ROWS // MLP_STEPS


def _mlp_kernel(xb_ref, xres_ref, w1_ref, w2_ref, g_ref, b_ref, xo_ref, xob_ref):
    f = pl.program_id(1)

    @pl.when(f == 0)
    def _():
        xo_ref[...] = jnp.zeros_like(xo_ref)

    h = jnp.square(jnp.maximum(_dot(xb_ref[...], w1_ref[...]), 0.0))
    xo_ref[...] += _dot(_bf(h), w2_ref[...])
    rows = pl.ds(pl.multiple_of(f * MLP_RES_ROWS, MLP_RES_ROWS), MLP_RES_ROWS)
    xo_ref[rows, :] += ALPHA * xres_ref[...]

    @pl.when(f == MLP_STEPS - 1)
    def _():
        y = _layer_norm_rows(xo_ref[...], g_ref[...], b_ref[...])
        xo_ref[...] = y
        xob_ref[...] = _bf(y)


def _mlp(x_bf, x, w1_bf, w2_bf, g, b, layer):
    m = x.shape[0]
    bm, bf = MLP_ROWS, MLP_FF
    row_spec = pl.BlockSpec((bm, D_MODEL), lambda i, f: (i, 0))
    vec_spec = pl.BlockSpec((None, 1, D_MODEL), lambda i, f: (layer, 0, 0))
    return pl.pallas_call(
        _mlp_kernel,
        grid=(m // bm, MLP_STEPS),
        in_specs=[row_spec,
                  pl.BlockSpec((MLP_RES_ROWS, D_MODEL), lambda i, f: (i * MLP_STEPS + f, 0)),
                  pl.BlockSpec((None, D_MODEL, bf), lambda i, f: (layer, 0, f)),
                  pl.BlockSpec((None, bf, D_MODEL), lambda i, f: (layer, f, 0)),
                  vec_spec, vec_spec],
        out_specs=[row_spec, row_spec],
        out_shape=[jax.ShapeDtypeStruct((m, D_MODEL), F32),
                   jax.ShapeDtypeStruct((m, D_MODEL), BF16)],
        compiler_params=pltpu.CompilerParams(
            dimension_semantics=("arbitrary", "arbitrary"), vmem_limit_bytes=VMEM_LIMIT),
        name="mlp_ln",
    )(x_bf, x, w1_bf, w2_bf, g, b)


def _pad_rows(w, rows, at=0):
    out = jnp.zeros((rows, w.shape[1]), w.dtype)
    return out.at[at:at + w.shape[0]].set(w)


def _layer_params(l, mu, w0, w_up, a0, a_up, g_up, v0, v_down, v_up, k_k, k_a, r_k,
                  gn_g, gn_b, pool_w, pool_scale):
    mu_p = jnp.concatenate([mu[l], jnp.zeros((C_USED - C_SHIFT,), F32)])[None, :]
    p = {
        "mu": mu_p,
        "w0": w0[l][None, :],
        "w_up": _bf(_pad_rows(w_up[l], LANE, 0)),
        "a0": a0[l][None, :],
        "a_up": _bf(_pad_rows(a_up[l], LANE, R_DECAY)),
        "g_up": _bf(_pad_rows(g_up[l], 2 * LANE, 0)),
        "k_k": k_k[l][None, :],
        "k_a": k_a[l][None, :],
        "pool_w": _bf(pool_w[l]),
        "pool_scale": pool_scale[l][None, :],
        "r_k": r_k[l].reshape(N_PAIRS, 1, LANE),
        "gn_g": gn_g[l].reshape(N_PAIRS, 1, LANE),
        "gn_b": gn_b[l].reshape(N_PAIRS, 1, LANE),
    }
    if l > 0:
        p["v0"] = v0[l - 1][None, :]
        p["v_down"] = _bf(jnp.pad(v_down[l - 1], ((0, 0), (0, LANE - R_MV))))
        p["v_up"] = _bf(_pad_rows(v_up[l - 1], LANE, 0))
    return p


def kernel(x, w_in, mu, w0, w_up, a0, a_up, g_up, v0, v_down, v_up, k_k, k_a, r_k, gn_g, gn_b,
           pool_w, pool_scale, w_out, ln1_g, ln1_b, mlp_w1, mlp_w2, ln2_g, ln2_b):
    m = BATCH * SEQ
    zc = jnp.zeros((DEPTH, D_MODEL, C_POOL - C_SHIFT), F32)
    zt = jnp.zeros((DEPTH, D_MODEL, C_PAD - C_USED), F32)
    w_in_bf = _bf(jnp.concatenate([w_in[:, :, :C_SHIFT], zc, w_in[:, :, C_SHIFT:], zt], axis=2))
    w_out_bf, w1_bf, w2_bf = _bf(w_out), _bf(mlp_w1), _bf(mlp_w2)
    ln1_g, ln1_b, ln2_g, ln2_b = (v.reshape(DEPTH, 1, D_MODEL) for v in (ln1_g, ln1_b, ln2_g, ln2_b))
    x_mm = x
    v_first = None
    for l in range(DEPTH):
        p = _layer_params(l, mu, w0, w_up, a0, a_up, g_up, v0, v_down, v_up, k_k, k_a,
                          r_k, gn_g, gn_b, pool_w, pool_scale)
        z = _in_proj(x_mm.reshape(m, D_MODEL), w_in_bf, l).reshape(BATCH, SEQ, C_PAD)
        r, k, v, an, bb, cum, g, y_pool = _prep(z, v_first, p, has_gate=l > 0)
        if l == 0:
            v_first = v
        y_rwkv = _recurrence(r, k, v, an, bb, cum, g, p["gn_g"], p["gn_b"], p["r_k"])
        x, x_bf = _out_proj(y_rwkv, y_pool, x, w_out_bf, ln1_g, ln1_b, l)
        x2, x2_bf = _mlp(x_bf.reshape(m, D_MODEL), x.reshape(m, D_MODEL), w1_bf, w2_bf,
                         ln2_g, ln2_b, l)
        x, x_mm = x2.reshape(BATCH, SEQ, D_MODEL), x2_bf.reshape(BATCH, SEQ, D_MODEL)
    return x
```

```python
import functools

import jax
import jax.numpy as jnp
from jax import lax
from jax.experimental import pallas as pl
from jax.experimental.pallas import tpu as pltpu

D_MODEL = 2048
BATCH = 2
SEQ = 8192
DEPTH = 2
RWKV_WIDTH = 1024
POOL_WIDTH = 1024
HEAD_SIZE = 64
POOL_WINDOWS = (2, 4, 8, 16)
POOL_GROUP = 256
R_DECAY = 64
R_AAA = 64
R_MV = 32
R_GATE = 160
C_SHIFT = 3 * RWKV_WIDTH + R_DECAY + R_AAA + R_GATE
C_IN = C_SHIFT + POOL_WIDTH
D_FF = 4 * D_MODEL
ALPHA = (2.0 * DEPTH) ** 0.25
LN_EPS = 1e-5
GN_EPS = 64e-5
L2_EPS = 1e-12

LANE = 128
CHUNK = 64
N_PAIRS = RWKV_WIDTH // LANE
MAX_WINDOW = max(POOL_WINDOWS)

C_LR = 3 * RWKV_WIDTH
C_LR_W = 3 * LANE
C_POOL = C_LR + C_LR_W
C_USED = C_POOL + POOL_WIDTH

VMEM_LIMIT = 52 * 1024 * 1024

F32 = jnp.float32
BF16 = jnp.bfloat16


def _bf(x):
    return x.astype(BF16)


def _dot(a, b):
    return jnp.dot(a, b, preferred_element_type=F32)


def _dot_nt(a, b):
    return lax.dot_general(a, b, (((1,), (1,)), ((), ())), preferred_element_type=F32)


def _dot_tn(a, b):
    return lax.dot_general(a, b, (((0,), (0,)), ((), ())), preferred_element_type=F32)


def _split_dot(x, w_bf, n_terms):
    acc = None
    rem = x
    for _ in range(n_terms):
        piece = _bf(rem)
        term = _dot(piece, w_bf)
        acc = term if acc is None else acc + term
        rem = rem - piece.astype(F32)
    return acc


def _split_dot_left(w_bf, x, n_terms):
    acc = None
    rem = x
    for _ in range(n_terms):
        piece = _bf(rem)
        term = _dot(w_bf, piece)
        acc = term if acc is None else acc + term
        rem = rem - piece.astype(F32)
    return acc


def _sigmoid(x):
    return 1.0 / (1.0 + jnp.exp(-x))


def _layer_norm_rows(h, g, b):
    mean = jnp.mean(h, axis=-1, keepdims=True)
    d = h - mean
    var = jnp.mean(d * d, axis=-1, keepdims=True)
    return d * lax.rsqrt(var + LN_EPS) * g + b


TM_ROWS = BATCH * CHUNK


def _prep_rows(z_ref, tail_ref, chunk, vf, prm, has_gate):
    t = CHUNK

    def token_shift_mix(lo, width):
        cur = z_ref[:, lo:lo + width]
        row = lax.broadcasted_iota(jnp.int32, cur.shape, 0)
        prev = pltpu.roll(cur, 1, axis=0)
        for b in range(BATCH):
            last = tail_ref[b, MAX_WINDOW - 1:MAX_WINDOW, lo:lo + width]
            prev = jnp.where(row == b * t, last, prev)
        return cur + (prev - cur) * prm["mu"][:, lo:lo + width]

    zl = token_shift_mix(C_LR, C_LR_W)
    xwa = zl[:, 0:LANE]
    xg = zl[:, LANE:C_LR_W]
    w_pre = prm["w0"][...] + _dot(_bf(jnp.tanh(xwa)), prm["w_up"][...])
    lw = -jnp.exp(-0.5) * _sigmoid(w_pre)
    a = _sigmoid(prm["a0"][...] + _dot(_bf(xwa), prm["a_up"][...]))
    g = _dot(_bf(_sigmoid(xg)), prm["g_up"][...])

    rr = lax.broadcasted_iota(jnp.int32, (TM_ROWS, TM_ROWS), 0)
    cc = lax.broadcasted_iota(jnp.int32, (TM_ROWS, TM_ROWS), 1)
    chunk_bits = t.bit_length() - 1
    same_chunk = jnp.right_shift(rr, chunk_bits) == jnp.right_shift(cc, chunk_bits)
    tri = jnp.where((rr >= cc) & same_chunk, 1.0, 0.0).astype(BF16)
    cum = _split_dot_left(tri, lw, 3)

    r = token_shift_mix(0, RWKV_WIDTH)
    k = token_shift_mix(RWKV_WIDTH, RWKV_WIDTH)
    v = token_shift_mix(2 * RWKV_WIDTH, RWKV_WIDTH)
    if has_gate:
        low = _dot(_bf(v), prm["v_down"][...])
        gate = _sigmoid(prm["v0"][...] + _dot(_bf(low), prm["v_up"][...]))
        v = v + (vf - v) * gate

    kk = k * prm["k_k"][...]
    sr = lax.broadcasted_iota(jnp.int32, (2 * LANE, 2 * LANE), 0)
    sc = lax.broadcasted_iota(jnp.int32, (2 * LANE, 2 * LANE), 1)
    head_bits = HEAD_SIZE.bit_length() - 1
    seg = jnp.where(jnp.right_shift(sr, head_bits) == jnp.right_shift(sc, head_bits),
                    1.0, 0.0).astype(BF16)
    sq = kk * kk
    ss = jnp.concatenate(
        [_split_dot(sq[:, j:j + 2 * LANE], seg, 3) for j in range(0, RWKV_WIDTH, 2 * LANE)], axis=1)
    kkn = kk / jnp.maximum(jnp.sqrt(ss), L2_EPS)
    an_w = -kkn * jnp.exp(-lw)
    bb = kkn * a
    k_mod = k * (1.0 + (a - 1.0) * prm["k_a"][...])

    pos = chunk * t + lax.broadcasted_iota(jnp.int32, (t, POOL_GROUP), 0) + 1
    outs = []
    for gi, win in enumerate(POOL_WINDOWS):
        lo = C_POOL + gi * POOL_GROUP
        cnt = jnp.minimum(pos, win).astype(F32)
        d_rows = []
        for b in range(BATCH):
            u = z_ref[b * t:(b + 1) * t, lo:lo + POOL_GROUP]
            e = jnp.concatenate([tail_ref[b, :, lo:lo + POOL_GROUP], u], axis=0)
            step = 1
            while step < win:
                e = e + pltpu.roll(e, step, axis=0)
                step *= 2
            d_rows.append(e[MAX_WINDOW:MAX_WINDOW + t, :] / cnt - u)
        outs.append(_dot(_bf(jnp.concatenate(d_rows, axis=0)), prm["pool_w"][gi]))
    y_pool = jnp.concatenate(outs, axis=1) * prm["pool_scale"][...]
    return r, k_mod, v, an_w, bb, cum, g, y_pool


def _recurrence_chunk(tiles, states, gn_g, gn_b, r_k):
    t = CHUNK
    n = 2 * t
    lane = lax.broadcasted_iota(jnp.int32, (t, LANE), 1)
    head0 = lane < HEAD_SIZE
    row = lax.broadcasted_iota(jnp.int32, (n, n), 0)
    col = lax.broadcasted_iota(jnp.int32, (n, n), 1)
    strict = row > col
    incl = row >= col
    own = (row < t) == (col < HEAD_SIZE)
    eye = jnp.where(row == col, 1.0, 0.0)

    def stack(x):
        return jnp.concatenate([jnp.where(head0, x, 0.0), jnp.where(head0, 0.0, x)], axis=0)

    def unstack(x):
        return x[:t] + x[t:]

    rs, ks, vs, ans, bbs, cums, gs = (list(x) for x in zip(*tiles))
    pairs = range(len(tiles))

    cum_ends = [c[t - 1:t, :] for c in cums]
    decay_in = [jnp.exp(c) for c in cums]
    decay_out = [jnp.exp(-c) for c in cums]
    decay_rest = [jnp.exp(e - c) for c, e in zip(cums, cum_ends)]
    v_st = [stack(v) for v in vs]
    v_m = [_bf(x) for x in v_st]
    ar = [jnp.concatenate([_bf(stack(a * d)), _bf(stack(r * d))], axis=0)
          for a, r, d in zip(ans, rs, decay_in)]
    bk = [jnp.concatenate([_bf(stack(b * d)), _bf(stack(k * d))], axis=0)
          for b, k, d in zip(bbs, ks, decay_out)]
    bk_end = [jnp.concatenate([_bf(stack(b * d)), _bf(stack(k * d))], axis=0)
              for b, k, d in zip(bbs, ks, decay_rest)]

    scores = [_dot_nt(x, y) for x, y in zip(ar, bk)]
    l_ab = [jnp.where(strict, sc[:n, :n], 0.0) for sc in scores]
    l_ak = [jnp.where(strict, sc[:n, n:], 0.0) for sc in scores]
    m_rbk = [jnp.concatenate([_bf(jnp.where(incl, sc[n:, :n], 0.0)),
                              _bf(jnp.where(incl, sc[n:, n:], 0.0))], axis=1) for sc in scores]
    from_state = [_dot_nt(x, _bf(st)) for x, st in zip(ar, states)]
    rhs = [fs[:n] + _dot(_bf(l), vm) for fs, l, vm in zip(from_state, l_ak, v_m)]

    power = [_bf(l) for l in l_ab]
    inv = [eye + l for l in l_ab]
    for _ in range(5):
        power = [_bf(_dot(x, x)) for x in power]
        inv = [x + _dot(_bf(x), pw) for x, pw in zip(inv, power)]
    u = [_dot(_bf(x), _bf(y)) for x, y in zip(inv, rhs)]
    uv = [jnp.concatenate([_bf(x), vm], axis=0) for x, vm in zip(u, v_m)]
    y_st = [fs[n:] + _dot(m, x) for fs, m, x in zip(from_state, m_rbk, uv)]
    new_states = [st * jnp.exp(e) + _dot_tn(x, be)
                  for st, e, x, be in zip(states, cum_ends, uv, bk_end)]

    outs = []
    for p in pairs:
        mean = jnp.sum(y_st[p], axis=-1, keepdims=True) * (1.0 / HEAD_SIZE)
        d = jnp.where(own, y_st[p] - mean, 0.0)
        var = jnp.sum(d * d, axis=-1, keepdims=True) * (1.0 / HEAD_SIZE)
        yn = d * lax.rsqrt(var + GN_EPS)
        bonus = jnp.sum(stack(rs[p] * ks[p] * r_k[p]), axis=-1, keepdims=True) * v_st[p]
        out = unstack(yn) * gn_g[p] + gn_b[p] + unstack(bonus)
        outs.append(out * gs[p])
    return outs, new_states


_PRM_NAMES = ("mu", "w0", "w_up", "a0", "a_up", "g_up", "k_k", "k_a", "pool_w", "pool_scale",
              "gn_g", "gn_b", "r_k")
_GATE_NAMES = ("v0", "v_down", "v_up")


def _timemix_kernel(*refs, has_gate):
    it = iter(refs)
    x_ref, w_ref = next(it), next(it)
    vf_ref = next(it) if has_gate else None
    names = _PRM_NAMES + (_GATE_NAMES if has_gate else ())
    prm = {nme: next(it) for nme in names}
    yr_ref, yp_ref = next(it), next(it)
    v_ref = None if has_gate else next(it)
    z_ref, tail_ref, s_ref = next(it), next(it), next(it)

    t = CHUNK
    c = pl.program_id(0)

    @pl.when(c == 0)
    def _():
        tail_ref[...] = jnp.zeros_like(tail_ref)
        s_ref[...] = jnp.zeros_like(s_ref)

    xb = jnp.concatenate([_bf(x_ref[b]) for b in range(BATCH)], axis=0)
    z_ref[...] = _dot(xb, w_ref[...])

    vf = None
    if has_gate:
        vf = jnp.concatenate([vf_ref[b] for b in range(BATCH)], axis=0)
    r, k, v, an_w, bb, cum, g, y_pool = _prep_rows(z_ref, tail_ref, c, vf, prm, has_gate)
    for b in range(BATCH):
        tail_ref[b] = z_ref[(b + 1) * t - MAX_WINDOW:(b + 1) * t, :]

    def pair_tile(x, b, p):
        return x[b * t:(b + 1) * t, p * LANE:(p + 1) * LANE]

    order = [(b, p) for b in range(BATCH) for p in range(N_PAIRS)]
    tiles = [tuple(pair_tile(x, b, p) for x in (r, k, v, an_w, bb, cum, g)) for b, p in order]
    states = [s_ref[i] for i in range(len(order))]
    lanes = [slice(p * LANE, (p + 1) * LANE) for _, p in order]
    outs, new_states = _recurrence_chunk(
        tiles, states,
        [prm["gn_g"][:, sl] for sl in lanes],
        [prm["gn_b"][:, sl] for sl in lanes],
        [prm["r_k"][:, sl] for sl in lanes])

    for i, (b, p) in enumerate(order):
        s_ref[i] = new_states[i]
        yr_ref[b, :, p * LANE:(p + 1) * LANE] = _bf(outs[i])
    for b in range(BATCH):
        yp_ref[b] = _bf(y_pool[b * t:(b + 1) * t, :])
        if not has_gate:
            v_ref[b] = v[b * t:(b + 1) * t, :]


def _time_mix(x, w_in_bf, vf, p, layer):
    has_gate = layer > 0
    t = CHUNK

    def rows(width):
        return pl.BlockSpec((BATCH, t, width), lambda c: (0, c, 0))

    def full(shape):
        return pl.BlockSpec(shape, lambda c: (0,) * len(shape))

    in_specs = [rows(D_MODEL),
                pl.BlockSpec((None, D_MODEL, C_USED), lambda c: (layer, 0, 0),
                             pipeline_mode=pl.Buffered(1))]
    args = [x, w_in_bf]
    if has_gate:
        in_specs.append(rows(RWKV_WIDTH))
        args.append(vf)
    for nme in _PRM_NAMES + (_GATE_NAMES if has_gate else ()):
        in_specs.append(full(p[nme].shape))
        args.append(p[nme])
    out_shape = [jax.ShapeDtypeStruct((BATCH, SEQ, RWKV_WIDTH), BF16),
                 jax.ShapeDtypeStruct((BATCH, SEQ, POOL_WIDTH), BF16)]
    out_specs = [rows(RWKV_WIDTH), rows(POOL_WIDTH)]
    if not has_gate:
        out_shape.append(jax.ShapeDtypeStruct((BATCH, SEQ, RWKV_WIDTH), F32))
        out_specs.append(rows(RWKV_WIDTH))
    return pl.pallas_call(
        functools.partial(_timemix_kernel, has_gate=has_gate),
        grid=(SEQ // t,),
        in_specs=in_specs,
        out_specs=out_specs,
        out_shape=out_shape,
        scratch_shapes=[pltpu.VMEM((TM_ROWS, C_USED), F32),
                        pltpu.VMEM((BATCH, MAX_WINDOW, C_USED), F32),
                        pltpu.VMEM((BATCH * N_PAIRS, 2 * HEAD_SIZE, LANE), F32)],
        compiler_params=pltpu.CompilerParams(
            dimension_semantics=("arbitrary",), vmem_limit_bytes=VMEM_LIMIT),
        name="time_mix_gate" if has_gate else "time_mix",
    )(*args)


OUT_ROWS = 512
OUT_SUB_ROWS = 128


def _outproj_kernel(yr_ref, yp_ref, x_ref, w_ref, g_ref, b_ref, xo_ref, xb_ref):
    groups = [pl.ds(s, OUT_SUB_ROWS) for s in range(0, OUT_ROWS, OUT_SUB_ROWS)]
    mixes = [_dot(yr_ref[0, rows, :], w_ref[0:RWKV_WIDTH, :])
             + _dot(yp_ref[0, rows, :], w_ref[RWKV_WIDTH:, :]) for rows in groups]
    ys = [_layer_norm_rows(ALPHA * x_ref[0, rows, :] + mix, g_ref[...], b_ref[...])
          for rows, mix in zip(groups, mixes)]
    for rows, y in zip(groups, ys):
        xo_ref[0, rows, :] = y
        xb_ref[0, rows, :] = _bf(y)


def _out_proj(yr, yp, x, w_bf, g, b, layer):
    bm = OUT_ROWS
    row_spec = pl.BlockSpec((1, bm, D_MODEL), lambda bi, i: (bi, i, 0))
    half_spec = pl.BlockSpec((1, bm, RWKV_WIDTH), lambda bi, i: (bi, i, 0))
    vec_spec = pl.BlockSpec((None, 1, D_MODEL), lambda bi, i: (layer, 0, 0))
    return pl.pallas_call(
        _outproj_kernel,
        grid=(BATCH, SEQ // bm),
        in_specs=[half_spec, half_spec, row_spec,
                  pl.BlockSpec((None, D_MODEL, D_MODEL), lambda bi, i: (layer, 0, 0)),
                  vec_spec, vec_spec],
        out_specs=[row_spec, row_spec],
        out_shape=[jax.ShapeDtypeStruct((BATCH, SEQ, D_MODEL), F32),
                   jax.ShapeDtypeStruct((BATCH, SEQ, D_MODEL), BF16)],
        compiler_params=pltpu.CompilerParams(
            dimension_semantics=("arbitrary", "arbitrary"), vmem_limit_bytes=VMEM_LIMIT),
        name="out_proj_ln",
    )(yr, yp, x, w_bf, g, b)


MLP_ROWS = 1024
MLP_FF = 512
MLP_STEPS = D_FF // MLP_FF
MLP_RES_ROWS = MLP_ROWS // MLP_STEPS


def _mlp_kernel(xb_ref, xres_ref, w1_ref, w2_ref, g_ref, b_ref, xo_ref, xob_ref):
    f = pl.program_id(1)

    @pl.when(f == 0)
    def _():
        xo_ref[...] = jnp.zeros_like(xo_ref)

    h = jnp.square(jnp.maximum(_dot(xb_ref[...], w1_ref[...]), 0.0))
    xo_ref[...] += _dot(_bf(h), w2_ref[...])
    rows = pl.ds(pl.multiple_of(f * MLP_RES_ROWS, MLP_RES_ROWS), MLP_RES_ROWS)
    xo_ref[rows, :] += ALPHA * xres_ref[...]

    @pl.when(f == MLP_STEPS - 1)
    def _():
        y = _layer_norm_rows(xo_ref[...], g_ref[...], b_ref[...])
        xo_ref[...] = y
        xob_ref[...] = _bf(y)


def _mlp(x_bf, x, w1_bf, w2_bf, g, b, layer):
    m = x.shape[0]
    bm, bf = MLP_ROWS, MLP_FF
    row_spec = pl.BlockSpec((bm, D_MODEL), lambda i, f: (i, 0))
    vec_spec = pl.BlockSpec((None, 1, D_MODEL), lambda i, f: (layer, 0, 0))
    return pl.pallas_call(
        _mlp_kernel,
        grid=(m // bm, MLP_STEPS),
        in_specs=[row_spec,
                  pl.BlockSpec((MLP_RES_ROWS, D_MODEL), lambda i, f: (i * MLP_STEPS + f, 0)),
                  pl.BlockSpec((None, D_MODEL, bf), lambda i, f: (layer, 0, f)),
                  pl.BlockSpec((None, bf, D_MODEL), lambda i, f: (layer, f, 0)),
                  vec_spec, vec_spec],
        out_specs=[row_spec, row_spec],
        out_shape=[jax.ShapeDtypeStruct((m, D_MODEL), F32),
                   jax.ShapeDtypeStruct((m, D_MODEL), BF16)],
        compiler_params=pltpu.CompilerParams(
            dimension_semantics=("arbitrary", "arbitrary"), vmem_limit_bytes=VMEM_LIMIT),
        name="mlp_ln",
    )(x_bf, x, w1_bf, w2_bf, g, b)


def _pad_rows(w, rows, at=0):
    out = jnp.zeros((rows, w.shape[1]), w.dtype)
    return out.at[at:at + w.shape[0]].set(w)


def _layer_params(l, mu, w0, w_up, a0, a_up, g_up, v0, v_down, v_up, k_k, k_a, r_k,
                  gn_g, gn_b, pool_w, pool_scale):
    p = {
        "mu": jnp.concatenate([mu[l], jnp.zeros((C_USED - C_SHIFT,), F32)])[None, :],
        "w0": w0[l][None, :],
        "w_up": _bf(_pad_rows(w_up[l], LANE, 0)),
        "a0": a0[l][None, :],
        "a_up": _bf(_pad_rows(a_up[l], LANE, R_DECAY)),
        "g_up": _bf(_pad_rows(g_up[l], 2 * LANE, 0)),
        "k_k": k_k[l][None, :],
        "k_a": k_a[l][None, :],
        "pool_w": _bf(pool_w[l]),
        "pool_scale": pool_scale[l][None, :],
        "r_k": r_k[l].reshape(1, RWKV_WIDTH),
        "gn_g": gn_g[l][None, :],
        "gn_b": gn_b[l][None, :],
    }
    if l > 0:
        p["v0"] = v0[l - 1][None, :]
        p["v_down"] = _bf(jnp.pad(v_down[l - 1], ((0, 0), (0, LANE - R_MV))))
        p["v_up"] = _bf(_pad_rows(v_up[l - 1], LANE, 0))
    return p


def kernel(x, w_in, mu, w0, w_up, a0, a_up, g_up, v0, v_down, v_up, k_k, k_a, r_k, gn_g, gn_b,
           pool_w, pool_scale, w_out, ln1_g, ln1_b, mlp_w1, mlp_w2, ln2_g, ln2_b):
    m = BATCH * SEQ
    w_bf = _bf(w_in)
    zc = jnp.zeros((DEPTH, D_MODEL, C_POOL - C_SHIFT), BF16)
    w_in_bf = jnp.concatenate([w_bf[:, :, :C_SHIFT], zc, w_bf[:, :, C_SHIFT:]], axis=2)
    w_out_bf, w1_bf, w2_bf = _bf(w_out), _bf(mlp_w1), _bf(mlp_w2)
    ln1_g, ln1_b, ln2_g, ln2_b = (v.reshape(DEPTH, 1, D_MODEL) for v in (ln1_g, ln1_b, ln2_g, ln2_b))
    x_mm = x
    v_first = None
    for l in range(DEPTH):
        p = _layer_params(l, mu, w0, w_up, a0, a_up, g_up, v0, v_down, v_up, k_k, k_a,
                          r_k, gn_g, gn_b, pool_w, pool_scale)
        res = _time_mix(x_mm, w_in_bf, v_first, p, l)
        y_rwkv, y_pool = res[0], res[1]
        if l == 0:
            v_first = res[2]
        x, x_bf = _out_proj(y_rwkv, y_pool, x, w_out_bf, ln1_g, ln1_b, l)
        x2, x2_bf = _mlp(x_bf.reshape(m, D_MODEL), x.reshape(m, D_MODEL), w1_bf, w2_bf,
                         ln2_g, ln2_b, l)
        x, x_mm = x2.reshape(BATCH, SEQ, D_MODEL), x2_bf.reshape(BATCH, SEQ, D_MODEL)
    return x
```

```python
import functools

import jax
import jax.numpy as jnp
from jax import lax
from jax.experimental import pallas as pl
from jax.experimental.pallas import tpu as pltpu

D_MODEL = 2048
BATCH = 2
SEQ = 8192
DEPTH = 2
RWKV_WIDTH = 1024
POOL_WIDTH = 1024
HEAD_SIZE = 64
POOL_WINDOWS = (2, 4, 8, 16)
POOL_GROUP = 256
R_DECAY = 64
R_AAA = 64
R_MV = 32
R_GATE = 160
C_SHIFT = 3 * RWKV_WIDTH + R_DECAY + R_AAA + R_GATE
C_IN = C_SHIFT + POOL_WIDTH
D_FF = 4 * D_MODEL
ALPHA = (2.0 * DEPTH) ** 0.25
LN_EPS = 1e-5
GN_EPS = 64e-5
L2_EPS = 1e-12

LANE = 128
CHUNK = 64
N_PAIRS = RWKV_WIDTH // LANE
MAX_WINDOW = max(POOL_WINDOWS)

C_LR = 3 * RWKV_WIDTH
C_LR_W = 3 * LANE
C_POOL = C_LR + C_LR_W
C_USED = C_POOL + POOL_WIDTH

VMEM_LIMIT = 56 * 1024 * 1024

F32 = jnp.float32
BF16 = jnp.bfloat16


def _bf(x):
    return x.astype(BF16)


def _dot(a, b):
    return jnp.dot(a, b, preferred_element_type=F32)


def _dot_nt(a, b):
    return lax.dot_general(a, b, (((1,), (1,)), ((), ())), preferred_element_type=F32)


def _dot_tn(a, b):
    return lax.dot_general(a, b, (((0,), (0,)), ((), ())), preferred_element_type=F32)


def _split_dot(x, w_bf, n_terms):
    acc = None
    rem = x
    for _ in range(n_terms):
        piece = _bf(rem)
        term = _dot(piece, w_bf)
        acc = term if acc is None else acc + term
        rem = rem - piece.astype(F32)
    return acc


def _split_dot_left(w_bf, x, n_terms):
    acc = None
    rem = x
    for _ in range(n_terms):
        piece = _bf(rem)
        term = _dot(w_bf, piece)
        acc = term if acc is None else acc + term
        rem = rem - piece.astype(F32)
    return acc


def _sigmoid(x):
    return 1.0 / (1.0 + jnp.exp(-x))


def _layer_norm_rows(h, g, b):
    mean = jnp.mean(h, axis=-1, keepdims=True)
    d = h - mean
    var = jnp.mean(d * d, axis=-1, keepdims=True)
    return d * lax.rsqrt(var + LN_EPS) * g + b


TM_ROWS = BATCH * CHUNK


def _prep_rows(z_ref, tail_ref, chunk, vf, prm, has_gate):
    t = CHUNK

    def token_shift_mix(lo, width):
        cur = z_ref[:, lo:lo + width]
        row = lax.broadcasted_iota(jnp.int32, cur.shape, 0)
        prev = pltpu.roll(cur, 1, axis=0)
        for b in range(BATCH):
            last = tail_ref[b, MAX_WINDOW - 1:MAX_WINDOW, lo:lo + width]
            prev = jnp.where(row == b * t, last, prev)
        return cur + (prev - cur) * prm["mu"][:, lo:lo + width]

    zl = token_shift_mix(C_LR, C_LR_W)
    xwa = zl[:, 0:LANE]
    xg = zl[:, LANE:C_LR_W]
    w_pre = prm["w0"][...] + _dot(_bf(jnp.tanh(xwa)), prm["w_up"][...])
    lw = -jnp.exp(-0.5) * _sigmoid(w_pre)
    a = _sigmoid(prm["a0"][...] + _dot(_bf(xwa), prm["a_up"][...]))
    g = _dot(_bf(_sigmoid(xg)), prm["g_up"][...])

    rr = lax.broadcasted_iota(jnp.int32, (TM_ROWS, TM_ROWS), 0)
    cc = lax.broadcasted_iota(jnp.int32, (TM_ROWS, TM_ROWS), 1)
    chunk_bits = t.bit_length() - 1
    same_chunk = jnp.right_shift(rr, chunk_bits) == jnp.right_shift(cc, chunk_bits)
    tri = jnp.where((rr >= cc) & same_chunk, 1.0, 0.0).astype(BF16)
    cum = _split_dot_left(tri, lw, 3)

    r = token_shift_mix(0, RWKV_WIDTH)
    k = token_shift_mix(RWKV_WIDTH, RWKV_WIDTH)
    v = token_shift_mix(2 * RWKV_WIDTH, RWKV_WIDTH)
    if has_gate:
        low = _dot(_bf(v), prm["v_down"][...])
        gate = _sigmoid(prm["v0"][...] + _dot(_bf(low), prm["v_up"][...]))
        v = v + (vf - v) * gate

    kk = k * prm["k_k"][...]
    sr = lax.broadcasted_iota(jnp.int32, (2 * LANE, 2 * LANE), 0)
    sc = lax.broadcasted_iota(jnp.int32, (2 * LANE, 2 * LANE), 1)
    head_bits = HEAD_SIZE.bit_length() - 1
    seg = jnp.where(jnp.right_shift(sr, head_bits) == jnp.right_shift(sc, head_bits),
                    1.0, 0.0).astype(BF16)
    sq = kk * kk
    ss = jnp.concatenate(
        [_split_dot(sq[:, j:j + 2 * LANE], seg, 3) for j in range(0, RWKV_WIDTH, 2 * LANE)], axis=1)
    kkn = kk / jnp.maximum(jnp.sqrt(ss), L2_EPS)
    an_w = -kkn * jnp.exp(-lw)
    bb = kkn * a
    k_mod = k * (1.0 + (a - 1.0) * prm["k_a"][...])

    pos = chunk * t + lax.broadcasted_iota(jnp.int32, (t, POOL_GROUP), 0) + 1
    outs = []
    for gi, win in enumerate(POOL_WINDOWS):
        lo = C_POOL + gi * POOL_GROUP
        cnt = jnp.minimum(pos, win).astype(F32)
        d_rows = []
        for b in range(BATCH):
            u = z_ref[b * t:(b + 1) * t, lo:lo + POOL_GROUP]
            e = jnp.concatenate([tail_ref[b, :, lo:lo + POOL_GROUP], u], axis=0)
            step = 1
            while step < win:
                e = e + pltpu.roll(e, step, axis=0)
                step *= 2
            d_rows.append(e[MAX_WINDOW:MAX_WINDOW + t, :] / cnt - u)
        outs.append(_dot(_bf(jnp.concatenate(d_rows, axis=0)), prm["pool_w"][gi]))
    y_pool = jnp.concatenate(outs, axis=1) * prm["pool_scale"][...]
    return r, k_mod, v, an_w, bb, cum, g, y_pool


def _recurrence_chunk(tiles, states, gn_g, gn_b, r_k):
    t = CHUNK
    n = 2 * t
    lane = lax.broadcasted_iota(jnp.int32, (t, LANE), 1)
    head0 = lane < HEAD_SIZE
    row = lax.broadcasted_iota(jnp.int32, (n, n), 0)
    col = lax.broadcasted_iota(jnp.int32, (n, n), 1)
    strict = row > col
    incl = row >= col
    own = (row < t) == (col < HEAD_SIZE)
    eye = jnp.where(row == col, 1.0, 0.0)

    def stack(x):
        return jnp.concatenate([jnp.where(head0, x, 0.0), jnp.where(head0, 0.0, x)], axis=0)

    def unstack(x):
        return x[:t] + x[t:]

    rs, ks, vs, ans, bbs, cums, gs = (list(x) for x in zip(*tiles))
    pairs = range(len(tiles))

    cum_ends = [c[t - 1:t, :] for c in cums]
    decay_in = [jnp.exp(c) for c in cums]
    decay_out = [jnp.exp(-c) for c in cums]
    decay_rest = [jnp.exp(e - c) for c, e in zip(cums, cum_ends)]
    v_st = [stack(v) for v in vs]
    v_m = [_bf(x) for x in v_st]
    ar = [jnp.concatenate([_bf(stack(a * d)), _bf(stack(r * d))], axis=0)
          for a, r, d in zip(ans, rs, decay_in)]
    bk = [jnp.concatenate([_bf(stack(b * d)), _bf(stack(k * d))], axis=0)
          for b, k, d in zip(bbs, ks, decay_out)]
    bk_end = [jnp.concatenate([_bf(stack(b * d)), _bf(stack(k * d))], axis=0)
              for b, k, d in zip(bbs, ks, decay_rest)]

    scores = [_dot_nt(x, y) for x, y in zip(ar, bk)]
    l_ab = [jnp.where(strict, sc[:n, :n], 0.0) for sc in scores]
    l_ak = [jnp.where(strict, sc[:n, n:], 0.0) for sc in scores]
    m_rbk = [jnp.concatenate([_bf(jnp.where(incl, sc[n:, :n], 0.0)),
                              _bf(jnp.where(incl, sc[n:, n:], 0.0))], axis=1) for sc in scores]
    from_state = [_dot_nt(x, _bf(st)) for x, st in zip(ar, states)]
    rhs = [fs[:n] + _dot(_bf(l), vm) for fs, l, vm in zip(from_state, l_ak, v_m)]

    power = [_bf(l) for l in l_ab]
    inv = [eye + l for l in l_ab]
    for _ in range(5):
        power = [_bf(_dot(x, x)) for x in power]
        inv = [x + _dot(_bf(x), pw) for x, pw in zip(inv, power)]
    u = [_dot(_bf(x), _bf(y)) for x, y in zip(inv, rhs)]
    uv = [jnp.concatenate([_bf(x), vm], axis=0) for x, vm in zip(u, v_m)]
    y_st = [fs[n:] + _dot(m, x) for fs, m, x in zip(from_state, m_rbk, uv)]
    new_states = [st * jnp.exp(e) + _dot_tn(x, be)
                  for st, e, x, be in zip(states, cum_ends, uv, bk_end)]

    outs = []
    for p in pairs:
        mean = jnp.sum(y_st[p], axis=-1, keepdims=True) * (1.0 / HEAD_SIZE)
        d = jnp.where(own, y_st[p] - mean, 0.0)
        var = jnp.sum(d * d, axis=-1, keepdims=True) * (1.0 / HEAD_SIZE)
        yn = d * lax.rsqrt(var + GN_EPS)
        bonus = jnp.sum(stack(rs[p] * ks[p] * r_k[p]), axis=-1, keepdims=True) * v_st[p]
        out = unstack(yn) * gn_g[p] + gn_b[p] + unstack(bonus)
        outs.append(out * gs[p])
    return outs, new_states


_PRM_NAMES = ("mu", "w0", "w_up", "a0", "a_up", "g_up", "k_k", "k_a", "pool_w", "pool_scale",
              "gn_g", "gn_b", "r_k")
_GATE_NAMES = ("v0", "v_down", "v_up")


def _timemix_kernel(*refs, has_gate):
    it = iter(refs)
    xres_ref = next(it)
    x_ref = next(it) if has_gate else xres_ref
    w_ref, wout_ref = next(it), next(it)
    vf_ref = next(it) if has_gate else None
    names = _PRM_NAMES + (_GATE_NAMES if has_gate else ())
    prm = {nme: next(it) for nme in names}
    lng_ref, lnb_ref = next(it), next(it)
    xo_ref, xob_ref = next(it), next(it)
    v_ref = None if has_gate else next(it)
    z_ref, tail_ref, s_ref = next(it), next(it), next(it)

    t = CHUNK
    c = pl.program_id(0)

    @pl.when(c == 0)
    def _():
        tail_ref[...] = jnp.zeros_like(tail_ref)
        s_ref[...] = jnp.zeros_like(s_ref)

    xb = jnp.concatenate([_bf(x_ref[b]) for b in range(BATCH)], axis=0)
    z_ref[...] = _dot(xb, w_ref[...])
    vf = None
    if has_gate:
        vf = jnp.concatenate([vf_ref[b] for b in range(BATCH)], axis=0)
    r, k, v, an_w, bb, cum, g, y_pool = _prep_rows(z_ref, tail_ref, c, vf, prm, has_gate)
    for b in range(BATCH):
        tail_ref[b] = z_ref[(b + 1) * t - MAX_WINDOW:(b + 1) * t, :]

    def pair_tile(x, b, p):
        return x[b * t:(b + 1) * t, p * LANE:(p + 1) * LANE]

    order = [(b, p) for b in range(BATCH) for p in range(N_PAIRS)]
    tiles = [tuple(pair_tile(x, b, p) for x in (r, k, v, an_w, bb, cum, g)) for b, p in order]
    states = [s_ref[i] for i in range(len(order))]
    lanes = [slice(p * LANE, (p + 1) * LANE) for _, p in order]
    outs, new_states = _recurrence_chunk(
        tiles, states,
        [prm["gn_g"][:, sl] for sl in lanes],
        [prm["gn_b"][:, sl] for sl in lanes],
        [prm["r_k"][:, sl] for sl in lanes])

    y_rwkv = jnp.concatenate(
        [jnp.concatenate([_bf(outs[b * N_PAIRS + p]) for p in range(N_PAIRS)], axis=1)
         for b in range(BATCH)], axis=0)
    mix = _dot(y_rwkv, wout_ref[0:RWKV_WIDTH, :]) + _dot(_bf(y_pool), wout_ref[RWKV_WIDTH:, :])
    xres = jnp.concatenate([xres_ref[b] for b in range(BATCH)], axis=0)
    x1 = _layer_norm_rows(ALPHA * xres + mix, lng_ref[...], lnb_ref[...])

    for i in range(len(order)):
        s_ref[i] = new_states[i]
    for b in range(BATCH):
        xo_ref[b] = x1[b * t:(b + 1) * t, :]
        xob_ref[b] = _bf(x1[b * t:(b + 1) * t, :])
        if not has_gate:
            v_ref[b] = v[b * t:(b + 1) * t, :]


def _time_mix(x_res, x_bf, w_in_bf, w_out_bf, vf, p, ln_g, ln_b, layer):
    has_gate = layer > 0
    t = CHUNK

    def rows(width):
        return pl.BlockSpec((BATCH, t, width), lambda c: (0, c, 0))

    def full(shape):
        return pl.BlockSpec(shape, lambda c: (0,) * len(shape))

    def layer_of(shape):
        return pl.BlockSpec((None,) + shape, lambda c: (layer,) + (0,) * len(shape),
                            pipeline_mode=pl.Buffered(1))

    in_specs = [rows(D_MODEL)] * (2 if has_gate else 1)
    args = [x_res, x_bf] if has_gate else [x_res]
    in_specs += [layer_of((D_MODEL, C_USED)), layer_of((D_MODEL, D_MODEL))]
    args += [w_in_bf, w_out_bf]
    if has_gate:
        in_specs.append(rows(RWKV_WIDTH))
        args.append(vf)
    for nme in _PRM_NAMES + (_GATE_NAMES if has_gate else ()):
        in_specs.append(full(p[nme].shape))
        args.append(p[nme])
    in_specs += [layer_of((1, D_MODEL))] * 2
    args += [ln_g, ln_b]
    out_shape = [jax.ShapeDtypeStruct((BATCH, SEQ, D_MODEL), F32),
                 jax.ShapeDtypeStruct((BATCH, SEQ, D_MODEL), BF16)]
    out_specs = [rows(D_MODEL), rows(D_MODEL)]
    if not has_gate:
        out_shape.append(jax.ShapeDtypeStruct((BATCH, SEQ, RWKV_WIDTH), F32))
        out_specs.append(rows(RWKV_WIDTH))
    return pl.pallas_call(
        functools.partial(_timemix_kernel, has_gate=has_gate),
        grid=(SEQ // t,),
        in_specs=in_specs,
        out_specs=out_specs,
        out_shape=out_shape,
        scratch_shapes=[pltpu.VMEM((TM_ROWS, C_USED), F32),
                        pltpu.VMEM((BATCH, MAX_WINDOW, C_USED), F32),
                        pltpu.VMEM((BATCH * N_PAIRS, 2 * HEAD_SIZE, LANE), F32)],
        compiler_params=pltpu.CompilerParams(
            dimension_semantics=("arbitrary",), vmem_limit_bytes=VMEM_LIMIT),
        name="time_mix_gate" if has_gate else "time_mix",
    )(*args)


MLP_ROWS = 1024
MLP_FF = 512
MLP_STEPS = D_FF // MLP_FF
MLP_RES_ROWS = MLP_ROWS // MLP_STEPS


def _mlp_kernel(xb_ref, xres_ref, w1_ref, w2_ref, g_ref, b_ref, xo_ref, xob_ref):
    f = pl.program_id(1)

    @pl.when(f == 0)
    def _():
        xo_ref[...] = jnp.zeros_like(xo_ref)

    h = jnp.square(jnp.maximum(_dot(xb_ref[...], _bf(w1_ref[...])), 0.0))
    xo_ref[...] += _dot(_bf(h), _bf(w2_ref[...]))
    rows = pl.ds(pl.multiple_of(f * MLP_RES_ROWS, MLP_RES_ROWS), MLP_RES_ROWS)
    xo_ref[rows, :] += ALPHA * xres_ref[...]

    @pl.when(f == MLP_STEPS - 1)
    def _():
        y = _layer_norm_rows(xo_ref[...], g_ref[...], b_ref[...])
        xo_ref[...] = y
        xob_ref[...] = _bf(y)


def _mlp(x_bf, x, w1_bf, w2_bf, g, b, layer):
    m = x.shape[0]
    bm, bf = MLP_ROWS, MLP_FF
    row_spec = pl.BlockSpec((bm, D_MODEL), lambda i, f: (i, 0))
    vec_spec = pl.BlockSpec((None, 1, D_MODEL), lambda i, f: (layer, 0, 0))
    return pl.pallas_call(
        _mlp_kernel,
        grid=(m // bm, MLP_STEPS),
        in_specs=[row_spec,
                  pl.BlockSpec((MLP_RES_ROWS, D_MODEL), lambda i, f: (i * MLP_STEPS + f, 0)),
                  pl.BlockSpec((None, D_MODEL, bf), lambda i, f: (layer, 0, f)),
                  pl.BlockSpec((None, bf, D_MODEL), lambda i, f: (layer, f, 0)),
                  vec_spec, vec_spec],
        out_specs=[row_spec, row_spec],
        out_shape=[jax.ShapeDtypeStruct((m, D_MODEL), F32),
                   jax.ShapeDtypeStruct((m, D_MODEL), BF16)],
        compiler_params=pltpu.CompilerParams(
            dimension_semantics=("arbitrary", "arbitrary"), vmem_limit_bytes=VMEM_LIMIT),
        name="mlp_ln",
    )(x_bf, x, w1_bf, w2_bf, g, b)


def _pad_rows(w, rows, at=0):
    out = jnp.zeros((rows, w.shape[1]), w.dtype)
    return out.at[at:at + w.shape[0]].set(w)


def _layer_params(l, mu, w0, w_up, a0, a_up, g_up, v0, v_down, v_up, k_k, k_a, r_k,
                  gn_g, gn_b, pool_w, pool_scale):
    p = {
        "mu": jnp.concatenate([mu[l], jnp.zeros((C_USED - C_SHIFT,), F32)])[None, :],
        "w0": w0[l][None, :],
        "w_up": _bf(_pad_rows(w_up[l], LANE, 0)),
        "a0": a0[l][None, :],
        "a_up": _bf(_pad_rows(a_up[l], LANE, R_DECAY)),
        "g_up": _bf(_pad_rows(g_up[l], 2 * LANE, 0)),
        "k_k": k_k[l][None, :],
        "k_a": k_a[l][None, :],
        "pool_w": _bf(pool_w[l]),
        "pool_scale": pool_scale[l][None, :],
        "r_k": r_k[l].reshape(1, RWKV_WIDTH),
        "gn_g": gn_g[l][None, :],
        "gn_b": gn_b[l][None, :],
    }
    if l > 0:
        p["v0"] = v0[l - 1][None, :]
        p["v_down"] = _bf(jnp.pad(v_down[l - 1], ((0, 0), (0, LANE - R_MV))))
        p["v_up"] = _bf(_pad_rows(v_up[l - 1], LANE, 0))
    return p


def kernel(x, w_in, mu, w0, w_up, a0, a_up, g_up, v0, v_down, v_up, k_k, k_a, r_k, gn_g, gn_b,
           pool_w, pool_scale, w_out, ln1_g, ln1_b, mlp_w1, mlp_w2, ln2_g, ln2_b):
    m = BATCH * SEQ
    w_in_bf = jnp.zeros((DEPTH, D_MODEL, C_USED), BF16)
    w_in_bf = lax.dynamic_update_slice(w_in_bf, _bf(w_in[:, :, :C_SHIFT]), (0, 0, 0))
    w_in_bf = lax.dynamic_update_slice(w_in_bf, _bf(w_in[:, :, C_SHIFT:]), (0, 0, C_POOL))
    w_out_bf, w1_bf, w2_bf = _bf(w_out), mlp_w1, mlp_w2
    ln1_g, ln1_b, ln2_g, ln2_b = (v.reshape(DEPTH, 1, D_MODEL) for v in (ln1_g, ln1_b, ln2_g, ln2_b))
    x_bf = None
    v_first = None
    for l in range(DEPTH):
        p = _layer_params(l, mu, w0, w_up, a0, a_up, g_up, v0, v_down, v_up, k_k, k_a,
                          r_k, gn_g, gn_b, pool_w, pool_scale)
        res = _time_mix(x, x_bf, w_in_bf, w_out_bf, v_first, p, ln1_g, ln1_b, l)
        x, x_bf = res[0], res[1]
        if l == 0:
            v_first = res[2]
        x2, x2_bf = _mlp(x_bf.reshape(m, D_MODEL), x.reshape(m, D_MODEL), w1_bf, w2_bf,
                         ln2_g, ln2_b, l)
        x, x_bf = x2.reshape(BATCH, SEQ, D_MODEL), x2_bf.reshape(BATCH, SEQ, D_MODEL)
    return x
```

```python
import functools

import jax
import jax.numpy as jnp
from jax import lax
from jax.experimental import pallas as pl
from jax.experimental.pallas import tpu as pltpu

D_MODEL = 2048
BATCH = 2
SEQ = 8192
DEPTH = 2
RWKV_WIDTH = 1024
POOL_WIDTH = 1024
HEAD_SIZE = 64
POOL_WINDOWS = (2, 4, 8, 16)
POOL_GROUP = 256
R_DECAY = 64
R_AAA = 64
R_MV = 32
R_GATE = 160
C_SHIFT = 3 * RWKV_WIDTH + R_DECAY + R_AAA + R_GATE
C_IN = C_SHIFT + POOL_WIDTH
D_FF = 4 * D_MODEL
ALPHA = (2.0 * DEPTH) ** 0.25
LN_EPS = 1e-5
GN_EPS = 64e-5
L2_EPS = 1e-12

LANE = 128
CHUNK = 64
N_PAIRS = RWKV_WIDTH // LANE
MAX_WINDOW = max(POOL_WINDOWS)

C_LR = 3 * RWKV_WIDTH
C_LR_W = 3 * LANE
C_POOL = C_LR + C_LR_W
C_USED = C_POOL + POOL_WIDTH

VMEM_LIMIT = 56 * 1024 * 1024

F32 = jnp.float32
BF16 = jnp.bfloat16


def _bf(x):
    return x.astype(BF16)


def _dot(a, b):
    return jnp.dot(a, b, preferred_element_type=F32)


def _dot_nt(a, b):
    return lax.dot_general(a, b, (((1,), (1,)), ((), ())), preferred_element_type=F32)


def _dot_tn(a, b):
    return lax.dot_general(a, b, (((0,), (0,)), ((), ())), preferred_element_type=F32)


def _split_dot(x, w_bf, n_terms):
    acc = None
    rem = x
    for _ in range(n_terms):
        piece = _bf(rem)
        term = _dot(piece, w_bf)
        acc = term if acc is None else acc + term
        rem = rem - piece.astype(F32)
    return acc


def _split_dot_left(w_bf, x, n_terms):
    acc = None
    rem = x
    for _ in range(n_terms):
        piece = _bf(rem)
        term = _dot(w_bf, piece)
        acc = term if acc is None else acc + term
        rem = rem - piece.astype(F32)
    return acc


def _sigmoid(x):
    return 1.0 / (1.0 + jnp.exp(-x))


def _layer_norm_rows(h, g, b):
    mean = jnp.mean(h, axis=-1, keepdims=True)
    d = h - mean
    var = jnp.mean(d * d, axis=-1, keepdims=True)
    return d * lax.rsqrt(var + LN_EPS) * g + b


TM_ROWS = BATCH * CHUNK


def _prep_rows(z_ref, tail_ref, chunk, vf, prm, has_gate):
    t = CHUNK

    def token_shift_mix(lo, width):
        cur = z_ref[:, lo:lo + width]
        row = lax.broadcasted_iota(jnp.int32, cur.shape, 0)
        prev = pltpu.roll(cur, 1, axis=0)
        for b in range(BATCH):
            last = tail_ref[b, MAX_WINDOW - 1:MAX_WINDOW, lo:lo + width]
            prev = jnp.where(row == b * t, last, prev)
        return cur + (prev - cur) * prm["mu"][:, lo:lo + width]

    zl = token_shift_mix(C_LR, C_LR_W)
    xwa = zl[:, 0:LANE]
    xg = zl[:, LANE:C_LR_W]
    w_pre = prm["w0"][...] + _dot(_bf(jnp.tanh(xwa)), prm["w_up"][...])
    lw = -jnp.exp(-0.5) * _sigmoid(w_pre)
    a = _sigmoid(prm["a0"][...] + _dot(_bf(xwa), prm["a_up"][...]))
    g = _dot(_bf(_sigmoid(xg)), prm["g_up"][...])

    rr = lax.broadcasted_iota(jnp.int32, (TM_ROWS, TM_ROWS), 0)
    cc = lax.broadcasted_iota(jnp.int32, (TM_ROWS, TM_ROWS), 1)
    chunk_bits = t.bit_length() - 1
    same_chunk = jnp.right_shift(rr, chunk_bits) == jnp.right_shift(cc, chunk_bits)
    tri = jnp.where((rr >= cc) & same_chunk, 1.0, 0.0).astype(BF16)
    cum = _split_dot_left(tri, lw, 3)

    r = token_shift_mix(0, RWKV_WIDTH)
    k = token_shift_mix(RWKV_WIDTH, RWKV_WIDTH)
    v = token_shift_mix(2 * RWKV_WIDTH, RWKV_WIDTH)
    if has_gate:
        low = _dot(_bf(v), prm["v_down"][...])
        gate = _sigmoid(prm["v0"][...] + _dot(_bf(low), prm["v_up"][...]))
        v = v + (vf - v) * gate

    kk = k * prm["k_k"][...]
    sr = lax.broadcasted_iota(jnp.int32, (2 * LANE, 2 * LANE), 0)
    sc = lax.broadcasted_iota(jnp.int32, (2 * LANE, 2 * LANE), 1)
    head_bits = HEAD_SIZE.bit_length() - 1
    seg = jnp.where(jnp.right_shift(sr, head_bits) == jnp.right_shift(sc, head_bits),
                    1.0, 0.0).astype(BF16)
    sq = kk * kk
    ss = jnp.concatenate(
        [_split_dot(sq[:, j:j + 2 * LANE], seg, 3) for j in range(0, RWKV_WIDTH, 2 * LANE)], axis=1)
    kkn = kk / jnp.maximum(jnp.sqrt(ss), L2_EPS)
    an_w = -kkn * jnp.exp(-lw)
    bb = kkn * a
    k_mod = k * (1.0 + (a - 1.0) * prm["k_a"][...])

    pos = chunk * t + lax.broadcasted_iota(jnp.int32, (t, POOL_GROUP), 0) + 1
    outs = []
    for gi, win in enumerate(POOL_WINDOWS):
        lo = C_POOL + gi * POOL_GROUP
        cnt = jnp.minimum(pos, win).astype(F32)
        d_rows = []
        for b in range(BATCH):
            u = z_ref[b * t:(b + 1) * t, lo:lo + POOL_GROUP]
            e = jnp.concatenate([tail_ref[b, :, lo:lo + POOL_GROUP], u], axis=0)
            step = 1
            while step < win:
                e = e + pltpu.roll(e, step, axis=0)
                step *= 2
            d_rows.append(e[MAX_WINDOW:MAX_WINDOW + t, :] / cnt - u)
        outs.append(_dot(_bf(jnp.concatenate(d_rows, axis=0)), prm["pool_w"][gi]))
    y_pool = jnp.concatenate(outs, axis=1) * prm["pool_scale"][...]
    return r, k_mod, v, an_w, bb, cum, g, y_pool


def _recurrence_chunk(tiles, states, gn_g, gn_b, r_k):
    t = CHUNK
    lane = lax.broadcasted_iota(jnp.int32, (t, LANE), 1)
    row = lax.broadcasted_iota(jnp.int32, (t, LANE), 0)
    head0 = lane < HEAD_SIZE
    col = jnp.bitwise_and(lane, t - 1)
    strict = row > col
    incl = row >= col
    eye = jnp.where(row == col, 1.0, 0.0)

    def stack(x):
        return jnp.concatenate([jnp.where(head0, x, 0.0), jnp.where(head0, 0.0, x)], axis=0)

    def head_sums(x):
        s0 = jnp.sum(jnp.where(head0, x, 0.0), axis=-1, keepdims=True)
        s1 = jnp.sum(jnp.where(head0, 0.0, x), axis=-1, keepdims=True)
        return jnp.where(head0, s0, s1)

    rs, ks, vs, ans, bbs, cums, gs = (list(x) for x in zip(*tiles))
    pairs = range(len(tiles))

    cum_ends = [c[t - 1:t, :] for c in cums]
    decay_in = [jnp.exp(c) for c in cums]
    decay_out = [jnp.exp(-c) for c in cums]
    decay_rest = [jnp.exp(e - c) for c, e in zip(cums, cum_ends)]
    v_st = [_bf(stack(v)) for v in vs]
    ar = [jnp.concatenate([_bf(a * d), _bf(r * d)], axis=0)
          for a, r, d in zip(ans, rs, decay_in)]
    bk_st = [jnp.concatenate([_bf(stack(b * d)), _bf(stack(k * d))], axis=0)
             for b, k, d in zip(bbs, ks, decay_out)]
    bk_end = [jnp.concatenate([_bf(stack(b * d)), _bf(stack(k * d))], axis=0)
              for b, k, d in zip(bbs, ks, decay_rest)]

    scores = [_dot_nt(x, y) for x, y in zip(ar, bk_st)]
    l_ab = [jnp.where(strict, sc[:t, :LANE], 0.0) for sc in scores]
    l_ak = [jnp.where(strict, sc[:t, LANE:], 0.0) for sc in scores]
    m_rbk = [jnp.concatenate([_bf(jnp.where(incl, sc[t:, :LANE], 0.0)),
                              _bf(jnp.where(incl, sc[t:, LANE:], 0.0))], axis=1) for sc in scores]
    from_state = [_dot_nt(x, _bf(st)) for x, st in zip(ar, states)]
    rhs = [fs[:t] + _dot(_bf(l), vst) for fs, l, vst in zip(from_state, l_ak, v_st)]

    power = l_ab
    inv = [eye + l for l in l_ab]
    for _ in range(5):
        power = [_dot(_bf(x), _bf(stack(x))) for x in power]
        inv = [x + _dot(_bf(x), _bf(stack(pw))) for x, pw in zip(inv, power)]
    u = [_dot(_bf(x), _bf(stack(y))) for x, y in zip(inv, rhs)]
    uv_st = [jnp.concatenate([_bf(stack(x)), vst], axis=0) for x, vst in zip(u, v_st)]
    ys = [fs[t:] + _dot(m, x) for fs, m, x in zip(from_state, m_rbk, uv_st)]
    new_states = [st * jnp.exp(e) + _dot_tn(x, be)
                  for st, e, x, be in zip(states, cum_ends, uv_st, bk_end)]

    outs = []
    for p in pairs:
        d = ys[p] - head_sums(ys[p]) * (1.0 / HEAD_SIZE)
        var = head_sums(d * d) * (1.0 / HEAD_SIZE)
        yn = d * lax.rsqrt(var + GN_EPS)
        bonus = head_sums(rs[p] * ks[p] * r_k[p]) * vs[p]
        outs.append((yn * gn_g[p] + gn_b[p] + bonus) * gs[p])
    return outs, new_states


_PRM_NAMES = ("mu", "w0", "w_up", "a0", "a_up", "g_up", "k_k", "k_a", "pool_w", "pool_scale",
              "gn_g", "gn_b", "r_k")
_GATE_NAMES = ("v0", "v_down", "v_up")


def _timemix_kernel(*refs, has_gate):
    it = iter(refs)
    xres_ref = next(it)
    x_ref = next(it) if has_gate else xres_ref
    w_ref, wout_ref = next(it), next(it)
    vf_ref = next(it) if has_gate else None
    names = _PRM_NAMES + (_GATE_NAMES if has_gate else ())
    prm = {nme: next(it) for nme in names}
    lng_ref, lnb_ref = next(it), next(it)
    xo_ref, xob_ref = next(it), next(it)
    v_ref = None if has_gate else next(it)
    z_ref, tail_ref, s_ref = next(it), next(it), next(it)

    t = CHUNK
    c = pl.program_id(0)

    @pl.when(c == 0)
    def _():
        tail_ref[...] = jnp.zeros_like(tail_ref)
        s_ref[...] = jnp.zeros_like(s_ref)

    xb = jnp.concatenate([_bf(x_ref[b]) for b in range(BATCH)], axis=0)
    z_ref[...] = _dot(xb, w_ref[...])
    vf = None
    if has_gate:
        vf = jnp.concatenate([vf_ref[b] for b in range(BATCH)], axis=0)
    r, k, v, an_w, bb, cum, g, y_pool = _prep_rows(z_ref, tail_ref, c, vf, prm, has_gate)
    for b in range(BATCH):
        tail_ref[b] = z_ref[(b + 1) * t - MAX_WINDOW:(b + 1) * t, :]

    def pair_tile(x, b, p):
        return x[b * t:(b + 1) * t, p * LANE:(p + 1) * LANE]

    order = [(b, p) for b in range(BATCH) for p in range(N_PAIRS)]
    tiles = [tuple(pair_tile(x, b, p) for x in (r, k, v, an_w, bb, cum, g)) for b, p in order]
    states = [s_ref[i] for i in range(len(order))]
    lanes = [slice(p * LANE, (p + 1) * LANE) for _, p in order]
    outs, new_states = _recurrence_chunk(
        tiles, states,
        [prm["gn_g"][:, sl] for sl in lanes],
        [prm["gn_b"][:, sl] for sl in lanes],
        [prm["r_k"][:, sl] for sl in lanes])

    y_rwkv = jnp.concatenate(
        [jnp.concatenate([_bf(outs[b * N_PAIRS + p]) for p in range(N_PAIRS)], axis=1)
         for b in range(BATCH)], axis=0)
    mix = _dot(y_rwkv, wout_ref[0:RWKV_WIDTH, :]) + _dot(_bf(y_pool), wout_ref[RWKV_WIDTH:, :])
    xres = jnp.concatenate([xres_ref[b] for b in range(BATCH)], axis=0)
    x1 = _layer_norm_rows(ALPHA * xres + mix, lng_ref[...], lnb_ref[...])

    for i in range(len(order)):
        s_ref[i] = new_states[i]
    for b in range(BATCH):
        xo_ref[b] = x1[b * t:(b + 1) * t, :]
        xob_ref[b] = _bf(x1[b * t:(b + 1) * t, :])
        if not has_gate:
            v_ref[b] = v[b * t:(b + 1) * t, :]


def _time_mix(x_res, x_bf, w_in_bf, w_out_bf, vf, p, ln_g, ln_b, layer):
    has_gate = layer > 0
    t = CHUNK

    def rows(width):
        return pl.BlockSpec((BATCH, t, width), lambda c: (0, c, 0))

    def full(shape):
        return pl.BlockSpec(shape, lambda c: (0,) * len(shape))

    def layer_of(shape):
        return pl.BlockSpec((None,) + shape, lambda c: (layer,) + (0,) * len(shape),
                            pipeline_mode=pl.Buffered(1))

    in_specs = [rows(D_MODEL)] * (2 if has_gate else 1)
    args = [x_res, x_bf] if has_gate else [x_res]
    in_specs += [layer_of((D_MODEL, C_USED)), layer_of((D_MODEL, D_MODEL))]
    args += [w_in_bf, w_out_bf]
    if has_gate:
        in_specs.append(rows(RWKV_WIDTH))
        args.append(vf)
    for nme in _PRM_NAMES + (_GATE_NAMES if has_gate else ()):
        in_specs.append(full(p[nme].shape))
        args.append(p[nme])
    in_specs += [layer_of((1, D_MODEL))] * 2
    args += [ln_g, ln_b]
    out_shape = [jax.ShapeDtypeStruct((BATCH, SEQ, D_MODEL), F32),
                 jax.ShapeDtypeStruct((BATCH, SEQ, D_MODEL), BF16)]
    out_specs = [rows(D_MODEL), rows(D_MODEL)]
    if not has_gate:
        out_shape.append(jax.ShapeDtypeStruct((BATCH, SEQ, RWKV_WIDTH), F32))
        out_specs.append(rows(RWKV_WIDTH))
    return pl.pallas_call(
        functools.partial(_timemix_kernel, has_gate=has_gate),
        grid=(SEQ // t,),
        in_specs=in_specs,
        out_specs=out_specs,
        out_shape=out_shape,
        scratch_shapes=[pltpu.VMEM((TM_ROWS, C_USED), F32),
                        pltpu.VMEM((BATCH, MAX_WINDOW, C_USED), F32),
                        pltpu.VMEM((BATCH * N_PAIRS, 2 * HEAD_SIZE, LANE), F32)],
        compiler_params=pltpu.CompilerParams(
            dimension_semantics=("arbitrary",), vmem_limit_bytes=VMEM_LIMIT),
        name="time_mix_gate" if has_gate else "time_mix",
    )(*args)


MLP_ROWS = 1024
MLP_FF = 512
MLP_STEPS = D_FF // MLP_FF
MLP_RES_ROWS = MLP_ROWS // MLP_STEPS


def _mlp_kernel(xb_ref, xres_ref, w1_ref, w2_ref, g_ref, b_ref, xo_ref, xob_ref):
    f = pl.program_id(1)

    @pl.when(f == 0)
    def _():
        xo_ref[...] = jnp.zeros_like(xo_ref)

    h = jnp.square(jnp.maximum(_dot(xb_ref[...], _bf(w1_ref[...])), 0.0))
    xo_ref[...] += _dot(_bf(h), _bf(w2_ref[...]))
    rows = pl.ds(pl.multiple_of(f * MLP_RES_ROWS, MLP_RES_ROWS), MLP_RES_ROWS)
    xo_ref[rows, :] += ALPHA * xres_ref[...]

    @pl.when(f == MLP_STEPS - 1)
    def _():
        y = _layer_norm_rows(xo_ref[...], g_ref[...], b_ref[...])
        xo_ref[...] = y
        xob_ref[...] = _bf(y)


def _mlp(x_bf, x, w1, w2, g, b, layer):
    m = x.shape[0]
    bm, bf = MLP_ROWS, MLP_FF
    row_spec = pl.BlockSpec((bm, D_MODEL), lambda i, f: (i, 0))
    vec_spec = pl.BlockSpec((None, 1, D_MODEL), lambda i, f: (layer, 0, 0))
    return pl.pallas_call(
        _mlp_kernel,
        grid=(m // bm, MLP_STEPS),
        in_specs=[row_spec,
                  pl.BlockSpec((MLP_RES_ROWS, D_MODEL), lambda i, f: (i * MLP_STEPS + f, 0)),
                  pl.BlockSpec((None, D_MODEL, bf), lambda i, f: (layer, 0, f)),
                  pl.BlockSpec((None, bf, D_MODEL), lambda i, f: (layer, f, 0)),
                  vec_spec, vec_spec],
        out_specs=[row_spec, row_spec],
        out_shape=[jax.ShapeDtypeStruct((m, D_MODEL), F32),
                   jax.ShapeDtypeStruct((m, D_MODEL), BF16)],
        compiler_params=pltpu.CompilerParams(
            dimension_semantics=("arbitrary", "arbitrary"), vmem_limit_bytes=VMEM_LIMIT),
        name="mlp_ln",
    )(x_bf, x, w1, w2, g, b)


def _pad_rows(w, rows, at=0):
    out = jnp.zeros((rows, w.shape[1]), w.dtype)
    return out.at[at:at + w.shape[0]].set(w)


def _layer_params(l, mu, w0, w_up, a0, a_up, g_up, v0, v_down, v_up, k_k, k_a, r_k,
                  gn_g, gn_b, pool_w, pool_scale):
    p = {
        "mu": jnp.concatenate([mu[l], jnp.zeros((C_USED - C_SHIFT,), F32)])[None, :],
        "w0": w0[l][None, :],
        "w_up": _bf(_pad_rows(w_up[l], LANE, 0)),
        "a0": a0[l][None, :],
        "a_up": _bf(_pad_rows(a_up[l], LANE, R_DECAY)),
        "g_up": _bf(_pad_rows(g_up[l], 2 * LANE, 0)),
        "k_k": k_k[l][None, :],
        "k_a": k_a[l][None, :],
        "pool_w": _bf(pool_w[l]),
        "pool_scale": pool_scale[l][None, :],
        "r_k": r_k[l].reshape(1, RWKV_WIDTH),
        "gn_g": gn_g[l][None, :],
        "gn_b": gn_b[l][None, :],
    }
    if l > 0:
        p["v0"] = v0[l - 1][None, :]
        p["v_down"] = _bf(jnp.pad(v_down[l - 1], ((0, 0), (0, LANE - R_MV))))
        p["v_up"] = _bf(_pad_rows(v_up[l - 1], LANE, 0))
    return p


def kernel(x, w_in, mu, w0, w_up, a0, a_up, g_up, v0, v_down, v_up, k_k, k_a, r_k, gn_g, gn_b,
           pool_w, pool_scale, w_out, ln1_g, ln1_b, mlp_w1, mlp_w2, ln2_g, ln2_b):
    m = BATCH * SEQ
    w_in_bf = jnp.zeros((DEPTH, D_MODEL, C_USED), BF16)
    w_in_bf = lax.dynamic_update_slice(w_in_bf, _bf(w_in[:, :, :C_SHIFT]), (0, 0, 0))
    w_in_bf = lax.dynamic_update_slice(w_in_bf, _bf(w_in[:, :, C_SHIFT:]), (0, 0, C_POOL))
    w_out_bf = _bf(w_out)
    ln1_g, ln1_b, ln2_g, ln2_b = (v.reshape(DEPTH, 1, D_MODEL) for v in (ln1_g, ln1_b, ln2_g, ln2_b))
    x_bf = None
    v_first = None
    for l in range(DEPTH):
        p = _layer_params(l, mu, w0, w_up, a0, a_up, g_up, v0, v_down, v_up, k_k, k_a,
                          r_k, gn_g, gn_b, pool_w, pool_scale)
        res = _time_mix(x, x_bf, w_in_bf, w_out_bf, v_first, p, ln1_g, ln1_b, l)
        x, x_bf = res[0], res[1]
        if l == 0:
            v_first = res[2]
        x2, x2_bf = _mlp(x_bf.reshape(m, D_MODEL), x.reshape(m, D_MODEL), mlp_w1, mlp_w2,
                         ln2_g, ln2_b, l)
        x, x_bf = x2.reshape(BATCH, SEQ, D_MODEL), x2_bf.reshape(BATCH, SEQ, D_MODEL)
    return x
```

```python
import functools

import jax
import jax.numpy as jnp
from jax import lax
from jax.experimental import pallas as pl
from jax.experimental.pallas import tpu as pltpu

D_MODEL = 2048
BATCH = 2
SEQ = 8192
DEPTH = 2
RWKV_WIDTH = 1024
POOL_WIDTH = 1024
HEAD_SIZE = 64
POOL_WINDOWS = (2, 4, 8, 16)
POOL_GROUP = 256
R_DECAY = 64
R_AAA = 64
R_MV = 32
R_GATE = 160
C_SHIFT = 3 * RWKV_WIDTH + R_DECAY + R_AAA + R_GATE
C_IN = C_SHIFT + POOL_WIDTH
D_FF = 4 * D_MODEL
ALPHA = (2.0 * DEPTH) ** 0.25
LN_EPS = 1e-5
GN_EPS = 64e-5
L2_EPS = 1e-12

LANE = 128
CHUNK = 64
N_PAIRS = RWKV_WIDTH // LANE
MAX_WINDOW = max(POOL_WINDOWS)

C_LR = 3 * RWKV_WIDTH
C_LR_W = 3 * LANE
C_POOL = C_LR + C_LR_W
C_USED = C_POOL + POOL_WIDTH

VMEM_LIMIT = 58 * 1024 * 1024

F32 = jnp.float32
BF16 = jnp.bfloat16


def _bf(x):
    return x.astype(BF16)


def _dot(a, b):
    return jnp.dot(a, b, preferred_element_type=F32)


def _dot_nt(a, b):
    return lax.dot_general(a, b, (((1,), (1,)), ((), ())), preferred_element_type=F32)


def _dot_tn(a, b):
    return lax.dot_general(a, b, (((0,), (0,)), ((), ())), preferred_element_type=F32)


def _split_dot(x, w_bf, n_terms):
    acc = None
    rem = x
    for _ in range(n_terms):
        piece = _bf(rem)
        term = _dot(piece, w_bf)
        acc = term if acc is None else acc + term
        rem = rem - piece.astype(F32)
    return acc


def _split_dot_left(w_bf, x, n_terms):
    acc = None
    rem = x
    for _ in range(n_terms):
        piece = _bf(rem)
        term = _dot(w_bf, piece)
        acc = term if acc is None else acc + term
        rem = rem - piece.astype(F32)
    return acc


def _sigmoid(x):
    return 1.0 / (1.0 + jnp.exp(-x))


def _layer_norm_rows(h, g, b):
    mean = jnp.mean(h, axis=-1, keepdims=True)
    d = h - mean
    var = jnp.mean(d * d, axis=-1, keepdims=True)
    return d * lax.rsqrt(var + LN_EPS) * g + b


TM_ROWS = BATCH * CHUNK
TM_CHUNKS = 2


def _prep_rows(z_ref, history, chunk, vf, prm, has_gate):
    t = CHUNK

    def token_shift_mix(lo, width):
        cur = z_ref[:, lo:lo + width]
        row = lax.broadcasted_iota(jnp.int32, cur.shape, 0)
        prev = pltpu.roll(cur, 1, axis=0)
        for b in range(BATCH):
            last = history(b, MAX_WINDOW - 1, MAX_WINDOW, lo, width)
            prev = jnp.where(row == b * t, last, prev)
        return cur + (prev - cur) * prm["mu"][:, lo:lo + width]

    zl = token_shift_mix(C_LR, C_LR_W)
    xwa = zl[:, 0:LANE]
    xg = zl[:, LANE:C_LR_W]
    w_pre = prm["w0"][...] + _dot(_bf(jnp.tanh(xwa)), prm["w_up"][...])
    lw = -jnp.exp(-0.5) * _sigmoid(w_pre)
    a = _sigmoid(prm["a0"][...] + _dot(_bf(xwa), prm["a_up"][...]))
    g = _dot(_bf(_sigmoid(xg)), prm["g_up"][...])

    rr = lax.broadcasted_iota(jnp.int32, (TM_ROWS, TM_ROWS), 0)
    cc = lax.broadcasted_iota(jnp.int32, (TM_ROWS, TM_ROWS), 1)
    chunk_bits = t.bit_length() - 1
    same_chunk = jnp.right_shift(rr, chunk_bits) == jnp.right_shift(cc, chunk_bits)
    tri = jnp.where((rr >= cc) & same_chunk, 1.0, 0.0).astype(BF16)
    cum = _split_dot_left(tri, lw, 3)

    r = token_shift_mix(0, RWKV_WIDTH)
    k = token_shift_mix(RWKV_WIDTH, RWKV_WIDTH)
    v = token_shift_mix(2 * RWKV_WIDTH, RWKV_WIDTH)
    if has_gate:
        low = _dot(_bf(v), prm["v_down"][...])
        gate = _sigmoid(prm["v0"][...] + _dot(_bf(low), prm["v_up"][...]))
        v = v + (vf - v) * gate

    kk = k * prm["k_k"][...]
    sr = lax.broadcasted_iota(jnp.int32, (2 * LANE, 2 * LANE), 0)
    sc = lax.broadcasted_iota(jnp.int32, (2 * LANE, 2 * LANE), 1)
    head_bits = HEAD_SIZE.bit_length() - 1
    seg = jnp.where(jnp.right_shift(sr, head_bits) == jnp.right_shift(sc, head_bits),
                    1.0, 0.0).astype(BF16)
    sq = kk * kk
    ss = jnp.concatenate(
        [_split_dot(sq[:, j:j + 2 * LANE], seg, 3) for j in range(0, RWKV_WIDTH, 2 * LANE)], axis=1)
    kkn = kk / jnp.maximum(jnp.sqrt(ss), L2_EPS)
    an_w = -kkn * jnp.exp(-lw)
    bb = kkn * a
    k_mod = k * (1.0 + (a - 1.0) * prm["k_a"][...])

    pos = chunk * t + lax.broadcasted_iota(jnp.int32, (t, POOL_GROUP), 0) + 1
    outs = []
    for gi, win in enumerate(POOL_WINDOWS):
        lo = C_POOL + gi * POOL_GROUP
        cnt = jnp.minimum(pos, win).astype(F32)
        d_rows = []
        for b in range(BATCH):
            u = z_ref[b * t:(b + 1) * t, lo:lo + POOL_GROUP]
            e = jnp.concatenate([history(b, 0, MAX_WINDOW, lo, POOL_GROUP), u], axis=0)
            step = 1
            while step < win:
                e = e + pltpu.roll(e, step, axis=0)
                step *= 2
            d_rows.append(e[MAX_WINDOW:MAX_WINDOW + t, :] / cnt - u)
        outs.append(_dot(_bf(jnp.concatenate(d_rows, axis=0)), prm["pool_w"][gi]))
    y_pool = jnp.concatenate(outs, axis=1) * prm["pool_scale"][...]
    return r, k_mod, v, an_w, bb, cum, g, y_pool


def _recurrence_chunk(tiles, states, gn_g, gn_b, r_k):
    t = CHUNK
    lane = lax.broadcasted_iota(jnp.int32, (t, LANE), 1)
    row = lax.broadcasted_iota(jnp.int32, (t, LANE), 0)
    head0 = lane < HEAD_SIZE
    col = jnp.bitwise_and(lane, t - 1)
    strict = row > col
    incl = row >= col
    eye = jnp.where(row == col, 1.0, 0.0)

    def stack(x):
        return jnp.concatenate([jnp.where(head0, x, 0.0), jnp.where(head0, 0.0, x)], axis=0)

    def head_sums(x):
        s0 = jnp.sum(jnp.where(head0, x, 0.0), axis=-1, keepdims=True)
        s1 = jnp.sum(jnp.where(head0, 0.0, x), axis=-1, keepdims=True)
        return jnp.where(head0, s0, s1)

    rs, ks, vs, ans, bbs, cums, gs = (list(x) for x in zip(*tiles))
    pairs = range(len(tiles))

    cum_ends = [c[t - 1:t, :] for c in cums]
    decay_in = [jnp.exp(c) for c in cums]
    decay_out = [jnp.exp(-c) for c in cums]
    decay_rest = [jnp.exp(e - c) for c, e in zip(cums, cum_ends)]
    v_st = [_bf(stack(v)) for v in vs]
    ar = [jnp.concatenate([_bf(a * d), _bf(r * d)], axis=0)
          for a, r, d in zip(ans, rs, decay_in)]
    bk_st = [jnp.concatenate([_bf(stack(b * d)), _bf(stack(k * d))], axis=0)
             for b, k, d in zip(bbs, ks, decay_out)]
    bk_end = [jnp.concatenate([_bf(stack(b * d)), _bf(stack(k * d))], axis=0)
              for b, k, d in zip(bbs, ks, decay_rest)]

    scores = [_dot_nt(x, y) for x, y in zip(ar, bk_st)]
    l_ab = [jnp.where(strict, sc[:t, :LANE], 0.0) for sc in scores]
    l_ak = [jnp.where(strict, sc[:t, LANE:], 0.0) for sc in scores]
    m_rbk = [jnp.concatenate([_bf(jnp.where(incl, sc[t:, :LANE], 0.0)),
                              _bf(jnp.where(incl, sc[t:, LANE:], 0.0))], axis=1) for sc in scores]
    from_state = [_dot_nt(x, _bf(st)) for x, st in zip(ar, states)]
    rhs = [fs[:t] + _dot(_bf(l), vst) for fs, l, vst in zip(from_state, l_ak, v_st)]

    power = l_ab
    inv = [eye + l for l in l_ab]
    for _ in range(5):
        power = [_dot(_bf(x), _bf(stack(x))) for x in power]
        inv = [x + _dot(_bf(x), _bf(stack(pw))) for x, pw in zip(inv, power)]
    u = [_dot(_bf(x), _bf(stack(y))) for x, y in zip(inv, rhs)]
    uv_st = [jnp.concatenate([_bf(stack(x)), vst], axis=0) for x, vst in zip(u, v_st)]
    ys = [fs[t:] + _dot(m, x) for fs, m, x in zip(from_state, m_rbk, uv_st)]
    new_states = [st * jnp.exp(e) + _dot_tn(x, be)
                  for st, e, x, be in zip(states, cum_ends, uv_st, bk_end)]

    outs = []
    for p in pairs:
        d = ys[p] - head_sums(ys[p]) * (1.0 / HEAD_SIZE)
        var = head_sums(d * d) * (1.0 / HEAD_SIZE)
        yn = d * lax.rsqrt(var + GN_EPS)
        bonus = head_sums(rs[p] * ks[p] * r_k[p]) * vs[p]
        outs.append((yn * gn_g[p] + gn_b[p] + bonus) * gs[p])
    return outs, new_states


_PRM_NAMES = ("mu", "w0", "w_up", "a0", "a_up", "g_up", "k_k", "k_a", "pool_w", "pool_scale",
              "gn_g", "gn_b", "r_k")
_GATE_NAMES = ("v0", "v_down", "v_up")


def _timemix_kernel(*refs, has_gate):
    it = iter(refs)
    xres_ref = next(it)
    x_ref = next(it) if has_gate else xres_ref
    w_ref, wout_ref = next(it), next(it)
    vf_ref = next(it) if has_gate else None
    names = _PRM_NAMES + (_GATE_NAMES if has_gate else ())
    prm = {nme: next(it) for nme in names}
    lng_ref, lnb_ref = next(it), next(it)
    xo_ref, xob_ref = next(it), next(it)
    v_ref = None if has_gate else next(it)
    z_ref, tail_ref, s_ref = next(it), next(it), next(it)

    t = CHUNK
    c = pl.program_id(0)

    @pl.when(c == 0)
    def _():
        tail_ref[...] = jnp.zeros_like(tail_ref)
        s_ref[...] = jnp.zeros_like(s_ref)

    slabs = [(s, b) for s in range(TM_CHUNKS) for b in range(BATCH)]

    def gather_rows(ref):
        return jnp.concatenate([ref[b, s * t:(s + 1) * t, :] for s, b in slabs], axis=0)

    z_ref[...] = _dot(_bf(gather_rows(x_ref)), w_ref[...])

    def pair_tile(x, b, p):
        return x[b * t:(b + 1) * t, p * LANE:(p + 1) * LANE]

    order = [(b, p) for b in range(BATCH) for p in range(N_PAIRS)]
    lanes = [slice(p * LANE, (p + 1) * LANE) for _, p in order]
    gn_g = [prm["gn_g"][:, sl] for sl in lanes]
    gn_b = [prm["gn_b"][:, sl] for sl in lanes]
    r_k = [prm["r_k"][:, sl] for sl in lanes]

    states = [s_ref[i] for i in range(len(order))]
    y_rwkv, y_pool, values = [], [], []
    for s in range(TM_CHUNKS):
        z_s = z_ref.at[s * TM_ROWS:(s + 1) * TM_ROWS]
        if s == 0:
            def history(b, r0, r1, lo, width):
                return tail_ref[b, r0:r1, lo:lo + width]
        else:
            def history(b, r0, r1, lo, width, base=(s - 1) * TM_ROWS):
                top = base + (b + 1) * t - MAX_WINDOW
                return z_ref[top + r0:top + r1, lo:lo + width]
        vf = None
        if has_gate:
            vf = jnp.concatenate([vf_ref[b, s * t:(s + 1) * t, :] for b in range(BATCH)], axis=0)
        r, k, v, an_w, bb, cum, g, pool = _prep_rows(z_s, history, c * TM_CHUNKS + s, vf, prm,
                                                     has_gate)
        tiles = [tuple(pair_tile(x, b, p) for x in (r, k, v, an_w, bb, cum, g)) for b, p in order]
        outs, states = _recurrence_chunk(tiles, states, gn_g, gn_b, r_k)
        y_rwkv += [jnp.concatenate([_bf(outs[b * N_PAIRS + p]) for p in range(N_PAIRS)], axis=1)
                   for b in range(BATCH)]
        y_pool.append(_bf(pool))
        values.append(v)

    last = (TM_CHUNKS - 1) * TM_ROWS
    for b in range(BATCH):
        tail_ref[b] = z_ref[last + (b + 1) * t - MAX_WINDOW:last + (b + 1) * t, :]

    mix = (_dot(jnp.concatenate(y_rwkv, axis=0), wout_ref[0:RWKV_WIDTH, :])
           + _dot(jnp.concatenate(y_pool, axis=0), wout_ref[RWKV_WIDTH:, :]))
    x1 = _layer_norm_rows(ALPHA * gather_rows(xres_ref) + mix, lng_ref[...], lnb_ref[...])

    for i in range(len(order)):
        s_ref[i] = states[i]
    for j, (s, b) in enumerate(slabs):
        xo_ref[b, s * t:(s + 1) * t, :] = x1[j * t:(j + 1) * t, :]
        xob_ref[b, s * t:(s + 1) * t, :] = _bf(x1[j * t:(j + 1) * t, :])
        if not has_gate:
            v_ref[b, s * t:(s + 1) * t, :] = values[s][b * t:(b + 1) * t, :]


def _time_mix(x_res, x_bf, w_in_bf, w_out_bf, vf, p, ln_g, ln_b, layer):
    has_gate = layer > 0
    t = CHUNK * TM_CHUNKS

    def rows(width):
        return pl.BlockSpec((BATCH, t, width), lambda c: (0, c, 0))

    def full(shape):
        return pl.BlockSpec(shape, lambda c: (0,) * len(shape))

    def layer_of(shape):
        return pl.BlockSpec((None,) + shape, lambda c: (layer,) + (0,) * len(shape),
                            pipeline_mode=pl.Buffered(1))

    in_specs = [rows(D_MODEL)] * (2 if has_gate else 1)
    args = [x_res, x_bf] if has_gate else [x_res]
    in_specs += [layer_of((D_MODEL, C_USED)), layer_of((D_MODEL, D_MODEL))]
    args += [w_in_bf, w_out_bf]
    if has_gate:
        in_specs.append(rows(RWKV_WIDTH))
        args.append(vf)
    for nme in _PRM_NAMES + (_GATE_NAMES if has_gate else ()):
        in_specs.append(full(p[nme].shape))
        args.append(p[nme])
    in_specs += [layer_of((1, D_MODEL))] * 2
    args += [ln_g, ln_b]
    out_shape = [jax.ShapeDtypeStruct((BATCH, SEQ, D_MODEL), F32),
                 jax.ShapeDtypeStruct((BATCH, SEQ, D_MODEL), BF16)]
    out_specs = [rows(D_MODEL), rows(D_MODEL)]
    if not has_gate:
        out_shape.append(jax.ShapeDtypeStruct((BATCH, SEQ, RWKV_WIDTH), F32))
        out_specs.append(rows(RWKV_WIDTH))
    return pl.pallas_call(
        functools.partial(_timemix_kernel, has_gate=has_gate),
        grid=(SEQ // t,),
        in_specs=in_specs,
        out_specs=out_specs,
        out_shape=out_shape,
        scratch_shapes=[pltpu.VMEM((TM_CHUNKS * TM_ROWS, C_USED), F32),
                        pltpu.VMEM((BATCH, MAX_WINDOW, C_USED), F32),
                        pltpu.VMEM((BATCH * N_PAIRS, 2 * HEAD_SIZE, LANE), F32)],
        compiler_params=pltpu.CompilerParams(
            dimension_semantics=("arbitrary",), vmem_limit_bytes=VMEM_LIMIT),
        name="time_mix_gate" if has_gate else "time_mix",
    )(*args)


MLP_ROWS = 1024
MLP_FF = 512
MLP_STEPS = D_FF // MLP_FF
MLP_RES_ROWS = MLP_ROWS // MLP_STEPS


def _mlp_kernel(xb_ref, xres_ref, w1_ref, w2_ref, g_ref, b_ref, xo_ref, xob_ref):
    f = pl.program_id(1)

    @pl.when(f == 0)
    def _():
        xo_ref[...] = jnp.zeros_like(xo_ref)

    h = jnp.square(jnp.maximum(_dot(xb_ref[...], _bf(w1_ref[...])), 0.0))
    xo_ref[...] += _dot(_bf(h), _bf(w2_ref[...]))
    rows = pl.ds(pl.multiple_of(f * MLP_RES_ROWS, MLP_RES_ROWS), MLP_RES_ROWS)
    xo_ref[rows, :] += ALPHA * xres_ref[...]

    @pl.when(f == MLP_STEPS - 1)
    def _():
        y = _layer_norm_rows(xo_ref[...], g_ref[...], b_ref[...])
        xo_ref[...] = y
        xob_ref[...] = _bf(y)


def _mlp(x_bf, x, w1, w2, g, b, layer):
    m = x.shape[0]
    bm, bf = MLP_ROWS, MLP_FF
    row_spec = pl.BlockSpec((bm, D_MODEL), lambda i, f: (i, 0))
    vec_spec = pl.BlockSpec((None, 1, D_MODEL), lambda i, f: (layer, 0, 0))
    return pl.pallas_call(
        _mlp_kernel,
        grid=(m // bm, MLP_STEPS),
        in_specs=[row_spec,
                  pl.BlockSpec((MLP_RES_ROWS, D_MODEL), lambda i, f: (i * MLP_STEPS + f, 0)),
                  pl.BlockSpec((None, D_MODEL, bf), lambda i, f: (layer, 0, f)),
                  pl.BlockSpec((None, bf, D_MODEL), lambda i, f: (layer, f, 0)),
                  vec_spec, vec_spec],
        out_specs=[row_spec, row_spec],
        out_shape=[jax.ShapeDtypeStruct((m, D_MODEL), F32),
                   jax.ShapeDtypeStruct((m, D_MODEL), BF16)],
        compiler_params=pltpu.CompilerParams(
            dimension_semantics=("arbitrary", "arbitrary"), vmem_limit_bytes=VMEM_LIMIT),
        name="mlp_ln",
    )(x_bf, x, w1, w2, g, b)


W_PREP_ROWS = 256


def _w_in_prep_kernel(w_ref, o_ref):
    o_ref[:, 0:C_SHIFT] = _bf(w_ref[:, 0:C_SHIFT])
    o_ref[:, C_SHIFT:C_POOL] = jnp.zeros((W_PREP_ROWS, C_POOL - C_SHIFT), BF16)
    o_ref[:, C_POOL:C_USED] = _bf(w_ref[:, C_SHIFT:C_IN])


def _w_in_prep(w_in):
    return pl.pallas_call(
        _w_in_prep_kernel,
        grid=(DEPTH, D_MODEL // W_PREP_ROWS),
        in_specs=[pl.BlockSpec((None, W_PREP_ROWS, C_IN), lambda l, i: (l, i, 0))],
        out_specs=pl.BlockSpec((None, W_PREP_ROWS, C_USED), lambda l, i: (l, i, 0)),
        out_shape=jax.ShapeDtypeStruct((DEPTH, D_MODEL, C_USED), BF16),
        compiler_params=pltpu.CompilerParams(
            dimension_semantics=("arbitrary", "arbitrary"), vmem_limit_bytes=VMEM_LIMIT),
        name="w_in_prep",
    )(w_in)


def _pad_rows(w, rows, at=0):
    out = jnp.zeros((rows, w.shape[1]), w.dtype)
    return out.at[at:at + w.shape[0]].set(w)


def _layer_params(l, mu, w0, w_up, a0, a_up, g_up, v0, v_down, v_up, k_k, k_a, r_k,
                  gn_g, gn_b, pool_w, pool_scale):
    p = {
        "mu": jnp.concatenate([mu[l], jnp.zeros((C_USED - C_SHIFT,), F32)])[None, :],
        "w0": w0[l][None, :],
        "w_up": _bf(_pad_rows(w_up[l], LANE, 0)),
        "a0": a0[l][None, :],
        "a_up": _bf(_pad_rows(a_up[l], LANE, R_DECAY)),
        "g_up": _bf(_pad_rows(g_up[l], 2 * LANE, 0)),
        "k_k": k_k[l][None, :],
        "k_a": k_a[l][None, :],
        "pool_w": _bf(pool_w[l]),
        "pool_scale": pool_scale[l][None, :],
        "r_k": r_k[l].reshape(1, RWKV_WIDTH),
        "gn_g": gn_g[l][None, :],
        "gn_b": gn_b[l][None, :],
    }
    if l > 0:
        p["v0"] = v0[l - 1][None, :]
        p["v_down"] = _bf(jnp.pad(v_down[l - 1], ((0, 0), (0, LANE - R_MV))))
        p["v_up"] = _bf(_pad_rows(v_up[l - 1], LANE, 0))
    return p


def kernel(x, w_in, mu, w0, w_up, a0, a_up, g_up, v0, v_down, v_up, k_k, k_a, r_k, gn_g, gn_b,
           pool_w, pool_scale, w_out, ln1_g, ln1_b, mlp_w1, mlp_w2, ln2_g, ln2_b):
    m = BATCH * SEQ
    w_in_bf = _w_in_prep(w_in)
    w_out_bf = _bf(w_out)
    ln1_g, ln1_b, ln2_g, ln2_b = (v.reshape(DEPTH, 1, D_MODEL) for v in (ln1_g, ln1_b, ln2_g, ln2_b))
    x_bf = None
    v_first = None
    for l in range(DEPTH):
        p = _layer_params(l, mu, w0, w_up, a0, a_up, g_up, v0, v_down, v_up, k_k, k_a,
                          r_k, gn_g, gn_b, pool_w, pool_scale)
        res = _time_mix(x, x_bf, w_in_bf, w_out_bf, v_first, p, ln1_g, ln1_b, l)
        x, x_bf = res[0], res[1]
        if l == 0:
            v_first = res[2]
        x2, x2_bf = _mlp(x_bf.reshape(m, D_MODEL), x.reshape(m, D_MODEL), mlp_w1, mlp_w2,
                         ln2_g, ln2_b, l)
        x, x_bf = x2.reshape(BATCH, SEQ, D_MODEL), x2_bf.reshape(BATCH, SEQ, D_MODEL)
    return x
```

```python
import functools
import math

import jax
import jax.numpy as jnp
from jax import lax
from jax.experimental import pallas as pl
from jax.experimental.pallas import tpu as pltpu

D_MODEL = 2048
BATCH = 2
SEQ = 8192
DEPTH = 2
RWKV_WIDTH = 1024
POOL_WIDTH = 1024
HEAD_SIZE = 64
POOL_WINDOWS = (2, 4, 8, 16)
POOL_GROUP = 256
R_DECAY = 64
R_AAA = 64
R_MV = 32
R_GATE = 160
C_SHIFT = 3 * RWKV_WIDTH + R_DECAY + R_AAA + R_GATE
C_IN = C_SHIFT + POOL_WIDTH
D_FF = 4 * D_MODEL
ALPHA = (2.0 * DEPTH) ** 0.25
LN_EPS = 1e-5
GN_EPS = 64e-5
L2_EPS = 1e-12

LANE = 128
CHUNK = 64
N_PAIRS = RWKV_WIDTH // LANE
MAX_WINDOW = max(POOL_WINDOWS)

C_LR = 3 * RWKV_WIDTH
C_LR_W = 3 * LANE
C_POOL = C_LR + C_LR_W
C_USED = C_POOL + POOL_WIDTH

VMEM_LIMIT = 58 * 1024 * 1024

F32 = jnp.float32
BF16 = jnp.bfloat16
F32_PIECES = 3


def _bf(x):
    return x.astype(BF16)


def _dot(a, b):
    return jnp.dot(a, b, preferred_element_type=F32)


def _dot_nt(a, b):
    return lax.dot_general(a, b, (((1,), (1,)), ((), ())), preferred_element_type=F32)


def _dot_tn(a, b):
    return lax.dot_general(a, b, (((0,), (0,)), ((), ())), preferred_element_type=F32)


def _split_dot(x, w_bf, n_terms):
    acc = None
    rem = x
    for _ in range(n_terms):
        piece = _bf(rem)
        term = _dot(piece, w_bf)
        acc = term if acc is None else acc + term
        rem = rem - piece.astype(F32)
    return acc


def _split_dot_left(w_bf, x, n_terms):
    acc = None
    rem = x
    for _ in range(n_terms):
        piece = _bf(rem)
        term = _dot(w_bf, piece)
        acc = term if acc is None else acc + term
        rem = rem - piece.astype(F32)
    return acc


def _sigmoid(x):
    return 1.0 / (1.0 + jnp.exp(-x))


def _layer_norm_rows(h, g, b):
    mean = jnp.mean(h, axis=-1, keepdims=True)
    d = h - mean
    var = jnp.mean(d * d, axis=-1, keepdims=True)
    return d * lax.rsqrt(var + LN_EPS) * g + b


TM_ROWS = BATCH * CHUNK
TM_CHUNKS = 2


def _prep_rows(z_ref, history, chunk, vf, prm, has_gate):
    t = CHUNK

    def token_shift_mix(lo, width):
        cur = z_ref[:, lo:lo + width]
        row = lax.broadcasted_iota(jnp.int32, cur.shape, 0)
        prev = pltpu.roll(cur, 1, axis=0)
        for b in range(BATCH):
            last = history(b, MAX_WINDOW - 1, MAX_WINDOW, lo, width)
            prev = jnp.where(row == b * t, last, prev)
        return cur + (prev - cur) * prm["mu"][:, lo:lo + width]

    zl = token_shift_mix(C_LR, C_LR_W)
    xwa = zl[:, 0:LANE]
    xg = zl[:, LANE:C_LR_W]
    w_pre = prm["w0"][...] + _dot(_bf(jnp.tanh(xwa)), prm["w_up"][...])
    lw = -jnp.exp(-0.5) * _sigmoid(w_pre)
    a = _sigmoid(prm["a0"][...] + _dot(_bf(xwa), prm["a_up"][...]))
    g = _dot(_bf(_sigmoid(xg)), prm["g_up"][...])

    rr = lax.broadcasted_iota(jnp.int32, (TM_ROWS, TM_ROWS), 0)
    cc = lax.broadcasted_iota(jnp.int32, (TM_ROWS, TM_ROWS), 1)
    chunk_bits = t.bit_length() - 1
    same_chunk = jnp.right_shift(rr, chunk_bits) == jnp.right_shift(cc, chunk_bits)
    tri = jnp.where((rr >= cc) & same_chunk, 1.0, 0.0).astype(BF16)
    cum = _split_dot_left(tri, lw, F32_PIECES)

    r = token_shift_mix(0, RWKV_WIDTH)
    k = token_shift_mix(RWKV_WIDTH, RWKV_WIDTH)
    v = token_shift_mix(2 * RWKV_WIDTH, RWKV_WIDTH)
    if has_gate:
        low = _dot(_bf(v), prm["v_down"][...])
        gate = _sigmoid(prm["v0"][...] + _dot(_bf(low), prm["v_up"][...]))
        v = v + (vf - v) * gate

    kk = k * prm["k_k"][...]
    sr = lax.broadcasted_iota(jnp.int32, (2 * LANE, 2 * LANE), 0)
    sc = lax.broadcasted_iota(jnp.int32, (2 * LANE, 2 * LANE), 1)
    head_bits = HEAD_SIZE.bit_length() - 1
    seg = jnp.where(jnp.right_shift(sr, head_bits) == jnp.right_shift(sc, head_bits),
                    1.0, 0.0).astype(BF16)
    sq = kk * kk
    ss = jnp.concatenate(
        [_split_dot(sq[:, j:j + 2 * LANE], seg, F32_PIECES)
         for j in range(0, RWKV_WIDTH, 2 * LANE)], axis=1)
    kkn = kk / jnp.maximum(jnp.sqrt(ss), L2_EPS)
    an_w = -kkn * jnp.exp(-lw)
    bb = kkn * a
    k_mod = k * (1.0 + (a - 1.0) * prm["k_a"][...])

    pos = chunk * t + lax.broadcasted_iota(jnp.int32, (t, POOL_GROUP), 0) + 1
    outs = []
    for gi, win in enumerate(POOL_WINDOWS):
        lo = C_POOL + gi * POOL_GROUP
        cnt = jnp.minimum(pos, win).astype(F32)
        d_rows = []
        for b in range(BATCH):
            u = z_ref[b * t:(b + 1) * t, lo:lo + POOL_GROUP]
            e = jnp.concatenate([history(b, 0, MAX_WINDOW, lo, POOL_GROUP), u], axis=0)
            step = 1
            while step < win:
                e = e + pltpu.roll(e, step, axis=0)
                step *= 2
            d_rows.append(e[MAX_WINDOW:MAX_WINDOW + t, :] / cnt - u)
        outs.append(_dot(_bf(jnp.concatenate(d_rows, axis=0)), prm["pool_w"][gi]))
    y_pool = jnp.concatenate(outs, axis=1) * prm["pool_scale"][...]
    return r, k_mod, v, an_w, bb, cum, g, y_pool


def _recurrence_chunk(tiles, states, gn_g, gn_b, r_k):
    t = CHUNK
    lane = lax.broadcasted_iota(jnp.int32, (t, LANE), 1)
    row = lax.broadcasted_iota(jnp.int32, (t, LANE), 0)
    head0 = lane < HEAD_SIZE
    col = jnp.bitwise_and(lane, t - 1)
    strict = row > col
    incl = row >= col
    eye = jnp.where(row == col, 1.0, 0.0)

    def stack(x):
        return jnp.concatenate([jnp.where(head0, x, 0.0), jnp.where(head0, 0.0, x)], axis=0)

    def head_sums(x):
        s0 = jnp.sum(jnp.where(head0, x, 0.0), axis=-1, keepdims=True)
        s1 = jnp.sum(jnp.where(head0, 0.0, x), axis=-1, keepdims=True)
        return jnp.where(head0, s0, s1)

    rs, ks, vs, ans, bbs, cums, gs = (list(x) for x in zip(*tiles))
    pairs = range(len(tiles))

    cum_ends = [c[t - 1:t, :] for c in cums]
    decay_in = [jnp.exp(c) for c in cums]
    decay_out = [jnp.exp(-c) for c in cums]
    decay_rest = [jnp.exp(e - c) for c, e in zip(cums, cum_ends)]
    v_st = [_bf(stack(v)) for v in vs]
    ar = [jnp.concatenate([_bf(a * d), _bf(r * d)], axis=0)
          for a, r, d in zip(ans, rs, decay_in)]
    bk_st = [jnp.concatenate([_bf(stack(b * d)), _bf(stack(k * d))], axis=0)
             for b, k, d in zip(bbs, ks, decay_out)]
    bk_end = [jnp.concatenate([_bf(stack(b * d)), _bf(stack(k * d))], axis=0)
              for b, k, d in zip(bbs, ks, decay_rest)]

    scores = [_dot_nt(x, y) for x, y in zip(ar, bk_st)]
    l_ab = [jnp.where(strict, sc[:t, :LANE], 0.0) for sc in scores]
    l_ak = [jnp.where(strict, sc[:t, LANE:], 0.0) for sc in scores]
    m_rbk = [jnp.concatenate([_bf(jnp.where(incl, sc[t:, :LANE], 0.0)),
                              _bf(jnp.where(incl, sc[t:, LANE:], 0.0))], axis=1) for sc in scores]
    from_state = [_dot_nt(x, _bf(st)) for x, st in zip(ar, states)]
    rhs = [fs[:t] + _dot(_bf(l), vst) for fs, l, vst in zip(from_state, l_ak, v_st)]

    power = l_ab
    inv = [eye + l for l in l_ab]
    for _ in range(5):
        power = [_dot(_bf(x), _bf(stack(x))) for x in power]
        inv = [x + _dot(_bf(x), _bf(stack(pw))) for x, pw in zip(inv, power)]
    u = [_dot(_bf(x), _bf(stack(y))) for x, y in zip(inv, rhs)]
    uv_st = [jnp.concatenate([_bf(stack(x)), vst], axis=0) for x, vst in zip(u, v_st)]
    ys = [fs[t:] + _dot(m, x) for fs, m, x in zip(from_state, m_rbk, uv_st)]
    new_states = [st * jnp.exp(e) + _dot_tn(x, be)
                  for st, e, x, be in zip(states, cum_ends, uv_st, bk_end)]

    outs = []
    for p in pairs:
        d = ys[p] - head_sums(ys[p]) * (1.0 / HEAD_SIZE)
        var = head_sums(d * d) * (1.0 / HEAD_SIZE)
        yn = d * lax.rsqrt(var + GN_EPS)
        bonus = head_sums(rs[p] * ks[p] * r_k[p]) * vs[p]
        outs.append((yn * gn_g[p] + gn_b[p] + bonus) * gs[p])
    return outs, new_states


_PRM_NAMES = ("mu", "w0", "w_up", "a0", "a_up", "g_up", "k_k", "k_a", "pool_w", "pool_scale",
              "gn_g", "gn_b", "r_k")
_GATE_NAMES = ("v0", "v_down", "v_up")


def _timemix_kernel(*refs, has_gate):
    it = iter(refs)
    xres_ref = next(it)
    x_ref = next(it) if has_gate else xres_ref
    w_ref, wout_ref = next(it), next(it)
    vf_ref = next(it) if has_gate else None
    names = _PRM_NAMES + (_GATE_NAMES if has_gate else ())
    prm = {nme: next(it) for nme in names}
    lng_ref, lnb_ref = next(it), next(it)
    xo_ref, xob_ref = next(it), next(it)
    v_ref = None if has_gate else next(it)
    z_ref, tail_ref, s_ref = next(it), next(it), next(it)

    t = CHUNK
    c = pl.program_id(0)

    @pl.when(c == 0)
    def _():
        tail_ref[...] = jnp.zeros_like(tail_ref)
        s_ref[...] = jnp.zeros_like(s_ref)

    slabs = [(s, b) for s in range(TM_CHUNKS) for b in range(BATCH)]

    def gather_rows(ref):
        return jnp.concatenate([ref[b, s * t:(s + 1) * t, :] for s, b in slabs], axis=0)

    z_ref[...] = _dot(_bf(gather_rows(x_ref)), w_ref[...])

    def pair_tile(x, b, p):
        return x[b * t:(b + 1) * t, p * LANE:(p + 1) * LANE]

    order = [(b, p) for b in range(BATCH) for p in range(N_PAIRS)]
    lanes = [slice(p * LANE, (p + 1) * LANE) for _, p in order]
    gn_g = [prm["gn_g"][:, sl] for sl in lanes]
    gn_b = [prm["gn_b"][:, sl] for sl in lanes]
    r_k = [prm["r_k"][:, sl] for sl in lanes]

    states = [s_ref[i] for i in range(len(order))]
    y_rwkv, y_pool, values = [], [], []
    for s in range(TM_CHUNKS):
        z_s = z_ref.at[s * TM_ROWS:(s + 1) * TM_ROWS]
        if s == 0:
            def history(b, r0, r1, lo, width):
                return tail_ref[b, r0:r1, lo:lo + width]
        else:
            def history(b, r0, r1, lo, width, base=(s - 1) * TM_ROWS):
                top = base + (b + 1) * t - MAX_WINDOW
                return z_ref[top + r0:top + r1, lo:lo + width]
        vf = None
        if has_gate:
            vf = jnp.concatenate([vf_ref[b, s * t:(s + 1) * t, :] for b in range(BATCH)], axis=0)
        r, k, v, an_w, bb, cum, g, pool = _prep_rows(z_s, history, c * TM_CHUNKS + s, vf, prm,
                                                     has_gate)
        tiles = [tuple(pair_tile(x, b, p) for x in (r, k, v, an_w, bb, cum, g)) for b, p in order]
        outs, states = _recurrence_chunk(tiles, states, gn_g, gn_b, r_k)
        y_rwkv += [jnp.concatenate([_bf(outs[b * N_PAIRS + p]) for p in range(N_PAIRS)], axis=1)
                   for b in range(BATCH)]
        y_pool.append(_bf(pool))
        values.append(v)

    last = (TM_CHUNKS - 1) * TM_ROWS
    for b in range(BATCH):
        tail_ref[b] = z_ref[last + (b + 1) * t - MAX_WINDOW:last + (b + 1) * t, :]

    mix = (_dot(jnp.concatenate(y_rwkv, axis=0), wout_ref[0:RWKV_WIDTH, :])
           + _dot(jnp.concatenate(y_pool, axis=0), wout_ref[RWKV_WIDTH:, :]))
    x1 = _layer_norm_rows(ALPHA * gather_rows(xres_ref) + mix, lng_ref[...], lnb_ref[...])

    for i in range(len(order)):
        s_ref[i] = states[i]
    for j, (s, b) in enumerate(slabs):
        xo_ref[b, s * t:(s + 1) * t, :] = x1[j * t:(j + 1) * t, :]
        xob_ref[b, s * t:(s + 1) * t, :] = _bf(x1[j * t:(j + 1) * t, :])
        if not has_gate:
            v_ref[b, s * t:(s + 1) * t, :] = values[s][b * t:(b + 1) * t, :]


def _time_mix(x_res, x_bf, w_in_bf, w_out_bf, vf, p, ln_g, ln_b, layer):
    has_gate = layer > 0
    t = CHUNK * TM_CHUNKS

    def rows(width):
        return pl.BlockSpec((BATCH, t, width), lambda c: (0, c, 0))

    def full(shape):
        return pl.BlockSpec(shape, lambda c: (0,) * len(shape))

    def layer_of(shape):
        return pl.BlockSpec((None,) + shape, lambda c: (layer,) + (0,) * len(shape),
                            pipeline_mode=pl.Buffered(1))

    in_specs = [rows(D_MODEL)] * (2 if has_gate else 1)
    args = [x_res, x_bf] if has_gate else [x_res]
    in_specs += [layer_of((D_MODEL, C_USED)), layer_of((D_MODEL, D_MODEL))]
    args += [w_in_bf, w_out_bf]
    if has_gate:
        in_specs.append(rows(RWKV_WIDTH))
        args.append(vf)
    for nme in _PRM_NAMES + (_GATE_NAMES if has_gate else ()):
        in_specs.append(full(p[nme].shape))
        args.append(p[nme])
    in_specs += [layer_of((1, D_MODEL))] * 2
    args += [ln_g, ln_b]
    out_shape = [jax.ShapeDtypeStruct((BATCH, SEQ, D_MODEL), F32),
                 jax.ShapeDtypeStruct((BATCH, SEQ, D_MODEL), BF16)]
    out_specs = [rows(D_MODEL), rows(D_MODEL)]
    if not has_gate:
        out_shape.append(jax.ShapeDtypeStruct((BATCH, SEQ, RWKV_WIDTH), F32))
        out_specs.append(rows(RWKV_WIDTH))
    return pl.pallas_call(
        functools.partial(_timemix_kernel, has_gate=has_gate),
        grid=(SEQ // t,),
        in_specs=in_specs,
        out_specs=out_specs,
        out_shape=out_shape,
        scratch_shapes=[pltpu.VMEM((TM_CHUNKS * TM_ROWS, C_USED), F32),
                        pltpu.VMEM((BATCH, MAX_WINDOW, C_USED), F32),
                        pltpu.VMEM((BATCH * N_PAIRS, 2 * HEAD_SIZE, LANE), F32)],
        compiler_params=pltpu.CompilerParams(
            dimension_semantics=("arbitrary",), vmem_limit_bytes=VMEM_LIMIT),
        name="time_mix_gate" if has_gate else "time_mix",
    )(*args)


MLP_ROWS = 1024
MLP_FF = 512
MLP_STEPS = D_FF // MLP_FF
MLP_RES_ROWS = MLP_ROWS // MLP_STEPS


def _mlp_kernel(xb_ref, xres_ref, w1_ref, w2_ref, g_ref, b_ref, xo_ref, xob_ref):
    f = pl.program_id(1)

    @pl.when(f == 0)
    def _():
        xo_ref[...] = jnp.zeros_like(xo_ref)

    h = jnp.square(jnp.maximum(_dot(xb_ref[...], _bf(w1_ref[...])), 0.0))
    xo_ref[...] += _dot(_bf(h), _bf(w2_ref[...]))
    rows = pl.ds(pl.multiple_of(f * MLP_RES_ROWS, MLP_RES_ROWS), MLP_RES_ROWS)
    xo_ref[rows, :] += ALPHA * xres_ref[...]

    @pl.when(f == MLP_STEPS - 1)
    def _():
        y = _layer_norm_rows(xo_ref[...], g_ref[...], b_ref[...])
        xo_ref[...] = y
        xob_ref[...] = _bf(y)


def _mlp(x_bf, x, w1, w2, g, b, layer):
    m = x.shape[0]
    bm, bf = MLP_ROWS, MLP_FF
    row_spec = pl.BlockSpec((bm, D_MODEL), lambda i, f: (i, 0))
    vec_spec = pl.BlockSpec((None, 1, D_MODEL), lambda i, f: (layer, 0, 0))
    return pl.pallas_call(
        _mlp_kernel,
        grid=(m // bm, MLP_STEPS),
        in_specs=[row_spec,
                  pl.BlockSpec((MLP_RES_ROWS, D_MODEL), lambda i, f: (i * MLP_STEPS + f, 0)),
                  pl.BlockSpec((None, D_MODEL, bf), lambda i, f: (layer, 0, f)),
                  pl.BlockSpec((None, bf, D_MODEL), lambda i, f: (layer, f, 0)),
                  vec_spec, vec_spec],
        out_specs=[row_spec, row_spec],
        out_shape=[jax.ShapeDtypeStruct((m, D_MODEL), F32),
                   jax.ShapeDtypeStruct((m, D_MODEL), BF16)],
        compiler_params=pltpu.CompilerParams(
            dimension_semantics=("arbitrary", "arbitrary"), vmem_limit_bytes=VMEM_LIMIT),
        name="mlp_ln",
    )(x_bf, x, w1, w2, g, b)


GAP_TILE = C_SHIFT // LANE
GAP_VALID = C_SHIFT - GAP_TILE * LANE


def _w_in_prep_kernel(wt_ref, o_ref):
    j = pl.program_id(1)
    tile = wt_ref[0].T
    col = lax.broadcasted_iota(jnp.int32, tile.shape, 1)
    tile = jnp.where((j == GAP_TILE) & (col >= GAP_VALID), 0.0, tile)
    o_ref[...] = _bf(tile)


def _w_in_prep(w_in):
    w_t = jnp.swapaxes(w_in, 1, 2)

    def source_row(j):
        shift = jnp.where(j > GAP_TILE, C_POOL - C_SHIFT, 0)
        return pl.multiple_of(j * LANE - shift, math.gcd(LANE, C_POOL - C_SHIFT))

    return pl.pallas_call(
        _w_in_prep_kernel,
        grid=(DEPTH, C_USED // LANE),
        in_specs=[pl.BlockSpec((pl.Element(1), pl.Element(LANE), pl.Element(D_MODEL)),
                               lambda l, j: (l, source_row(j), 0))],
        out_specs=pl.BlockSpec((None, D_MODEL, LANE), lambda l, j: (l, 0, j)),
        out_shape=jax.ShapeDtypeStruct((DEPTH, D_MODEL, C_USED), BF16),
        compiler_params=pltpu.CompilerParams(
            dimension_semantics=("arbitrary", "arbitrary"), vmem_limit_bytes=VMEM_LIMIT),
        name="w_in_prep",
    )(w_t)


def _pad_rows(w, rows, at=0):
    out = jnp.zeros((rows, w.shape[1]), w.dtype)
    return out.at[at:at + w.shape[0]].set(w)


def _layer_params(l, mu, w0, w_up, a0, a_up, g_up, v0, v_down, v_up, k_k, k_a, r_k,
                  gn_g, gn_b, pool_w, pool_scale):
    p = {
        "mu": jnp.concatenate([mu[l], jnp.zeros((C_USED - C_SHIFT,), F32)])[None, :],
        "w0": w0[l][None, :],
        "w_up": _bf(_pad_rows(w_up[l], LANE, 0)),
        "a0": a0[l][None, :],
        "a_up": _bf(_pad_rows(a_up[l], LANE, R_DECAY)),
        "g_up": _bf(_pad_rows(g_up[l], 2 * LANE, 0)),
        "k_k": k_k[l][None, :],
        "k_a": k_a[l][None, :],
        "pool_w": _bf(pool_w[l]),
        "pool_scale": pool_scale[l][None, :],
        "r_k": r_k[l].reshape(1, RWKV_WIDTH),
        "gn_g": gn_g[l][None, :],
        "gn_b": gn_b[l][None, :],
    }
    if l > 0:
        p["v0"] = v0[l - 1][None, :]
        p["v_down"] = _bf(jnp.pad(v_down[l - 1], ((0, 0), (0, LANE - R_MV))))
        p["v_up"] = _bf(_pad_rows(v_up[l - 1], LANE, 0))
    return p


def kernel(x, w_in, mu, w0, w_up, a0, a_up, g_up, v0, v_down, v_up, k_k, k_a, r_k, gn_g, gn_b,
           pool_w, pool_scale, w_out, ln1_g, ln1_b, mlp_w1, mlp_w2, ln2_g, ln2_b):
    m = BATCH * SEQ
    w_in_bf = _w_in_prep(w_in)
    w_out_bf = _bf(w_out)
    ln1_g, ln1_b, ln2_g, ln2_b = (v.reshape(DEPTH, 1, D_MODEL) for v in (ln1_g, ln1_b, ln2_g, ln2_b))
    x_bf = None
    v_first = None
    for l in range(DEPTH):
        p = _layer_params(l, mu, w0, w_up, a0, a_up, g_up, v0, v_down, v_up, k_k, k_a,
                          r_k, gn_g, gn_b, pool_w, pool_scale)
        res = _time_mix(x, x_bf, w_in_bf, w_out_bf, v_first, p, ln1_g, ln1_b, l)
        x, x_bf = res[0], res[1]
        if l == 0:
            v_first = res[2]
        x2, x2_bf = _mlp(x_bf.reshape(m, D_MODEL), x.reshape(m, D_MODEL), mlp_w1, mlp_w2,
                         ln2_g, ln2_b, l)
        x, x_bf = x2.reshape(BATCH, SEQ, D_MODEL), x2_bf.reshape(BATCH, SEQ, D_MODEL)
    return x
```

```python
import functools
import math

import jax
import jax.numpy as jnp
from jax import lax
from jax.experimental import pallas as pl
from jax.experimental.pallas import tpu as pltpu

D_MODEL = 2048
BATCH = 2
SEQ = 8192
DEPTH = 2
RWKV_WIDTH = 1024
POOL_WIDTH = 1024
HEAD_SIZE = 64
POOL_WINDOWS = (2, 4, 8, 16)
POOL_GROUP = 256
R_DECAY = 64
R_AAA = 64
R_MV = 32
R_GATE = 160
C_SHIFT = 3 * RWKV_WIDTH + R_DECAY + R_AAA + R_GATE
C_IN = C_SHIFT + POOL_WIDTH
D_FF = 4 * D_MODEL
ALPHA = (2.0 * DEPTH) ** 0.25
LN_EPS = 1e-5
GN_EPS = 64e-5
L2_EPS = 1e-12

LANE = 128
CHUNK = 64
N_PAIRS = RWKV_WIDTH // LANE
MAX_WINDOW = max(POOL_WINDOWS)

C_LR = 3 * RWKV_WIDTH
C_LR_W = 3 * LANE
C_POOL = C_LR + C_LR_W
C_USED = C_POOL + POOL_WIDTH

VMEM_LIMIT = 58 * 1024 * 1024

F32 = jnp.float32
BF16 = jnp.bfloat16
F32_PIECES = 3


def _bf(x):
    return x.astype(BF16)


def _dot(a, b):
    return jnp.dot(a, b, preferred_element_type=F32)


def _dot_nt(a, b):
    return lax.dot_general(a, b, (((1,), (1,)), ((), ())), preferred_element_type=F32)


def _dot_tn(a, b):
    return lax.dot_general(a, b, (((0,), (0,)), ((), ())), preferred_element_type=F32)


def _split_dot(x, w_bf, n_terms):
    acc = None
    rem = x
    for _ in range(n_terms):
        piece = _bf(rem)
        term = _dot(piece, w_bf)
        acc = term if acc is None else acc + term
        rem = rem - piece.astype(F32)
    return acc


def _split_dot_left(w_bf, x, n_terms):
    acc = None
    rem = x
    for _ in range(n_terms):
        piece = _bf(rem)
        term = _dot(w_bf, piece)
        acc = term if acc is None else acc + term
        rem = rem - piece.astype(F32)
    return acc


def _sigmoid(x):
    return 1.0 / (1.0 + jnp.exp(-x))


def _layer_norm_rows(h, g, b):
    mean = jnp.mean(h, axis=-1, keepdims=True)
    d = h - mean
    var = jnp.mean(d * d, axis=-1, keepdims=True)
    return d * lax.rsqrt(var + LN_EPS) * g + b


TM_ROWS = BATCH * CHUNK
TM_CHUNKS = 2


def _prep_rows(z_ref, history, chunk, vf, prm, has_gate):
    t = CHUNK

    def token_shift_mix(lo, width):
        cur = z_ref[:, lo:lo + width]
        row = lax.broadcasted_iota(jnp.int32, cur.shape, 0)
        prev = pltpu.roll(cur, 1, axis=0)
        for b in range(BATCH):
            last = history(b, MAX_WINDOW - 1, MAX_WINDOW, lo, width)
            prev = jnp.where(row == b * t, last, prev)
        return cur + (prev - cur) * prm["mu"][:, lo:lo + width]

    zl = token_shift_mix(C_LR, C_LR_W)
    xwa = zl[:, 0:LANE]
    xg = zl[:, LANE:C_LR_W]
    w_pre = prm["w0"][...] + _dot(_bf(jnp.tanh(xwa)), prm["w_up"][...])
    lw = -jnp.exp(-0.5) * _sigmoid(w_pre)
    a = _sigmoid(prm["a0"][...] + _dot(_bf(xwa), prm["a_up"][...]))
    g = _dot(_bf(_sigmoid(xg)), prm["g_up"][...])

    rr = lax.broadcasted_iota(jnp.int32, (TM_ROWS, TM_ROWS), 0)
    cc = lax.broadcasted_iota(jnp.int32, (TM_ROWS, TM_ROWS), 1)
    chunk_bits = t.bit_length() - 1
    same_chunk = jnp.right_shift(rr, chunk_bits) == jnp.right_shift(cc, chunk_bits)
    tri = jnp.where((rr >= cc) & same_chunk, 1.0, 0.0).astype(BF16)
    cum = _split_dot_left(tri, lw, F32_PIECES)

    r = token_shift_mix(0, RWKV_WIDTH)
    k = token_shift_mix(RWKV_WIDTH, RWKV_WIDTH)
    v = token_shift_mix(2 * RWKV_WIDTH, RWKV_WIDTH)
    if has_gate:
        low = _dot(_bf(v), prm["v_down"][...])
        gate = _sigmoid(prm["v0"][...] + _dot(_bf(low), prm["v_up"][...]))
        v = v + (vf - v) * gate

    kk = k * prm["k_k"][...]
    sr = lax.broadcasted_iota(jnp.int32, (2 * LANE, 2 * LANE), 0)
    sc = lax.broadcasted_iota(jnp.int32, (2 * LANE, 2 * LANE), 1)
    head_bits = HEAD_SIZE.bit_length() - 1
    seg = jnp.where(jnp.right_shift(sr, head_bits) == jnp.right_shift(sc, head_bits),
                    1.0, 0.0).astype(BF16)
    sq = kk * kk
    ss = jnp.concatenate(
        [_split_dot(sq[:, j:j + 2 * LANE], seg, F32_PIECES)
         for j in range(0, RWKV_WIDTH, 2 * LANE)], axis=1)
    kkn = kk / jnp.maximum(jnp.sqrt(ss), L2_EPS)
    an_w = -kkn * jnp.exp(-lw)
    bb = kkn * a
    k_mod = k * (1.0 + (a - 1.0) * prm["k_a"][...])

    pos = chunk * t + lax.broadcasted_iota(jnp.int32, (t, POOL_GROUP), 0) + 1
    outs = []
    for gi, win in enumerate(POOL_WINDOWS):
        lo = C_POOL + gi * POOL_GROUP
        cnt = jnp.minimum(pos, win).astype(F32)
        d_rows = []
        for b in range(BATCH):
            u = z_ref[b * t:(b + 1) * t, lo:lo + POOL_GROUP]
            e = jnp.concatenate([history(b, 0, MAX_WINDOW, lo, POOL_GROUP), u], axis=0)
            step = 1
            while step < win:
                e = e + pltpu.roll(e, step, axis=0)
                step *= 2
            d_rows.append(e[MAX_WINDOW:MAX_WINDOW + t, :] / cnt - u)
        outs.append(_dot(_bf(jnp.concatenate(d_rows, axis=0)), prm["pool_w"][gi]))
    y_pool = jnp.concatenate(outs, axis=1) * prm["pool_scale"][...]
    return r, k_mod, v, an_w, bb, cum, g, y_pool


def _recurrence_chunk(tiles, states, gn_g, gn_b, r_k):
    t = CHUNK
    lane = lax.broadcasted_iota(jnp.int32, (t, LANE), 1)
    row = lax.broadcasted_iota(jnp.int32, (t, LANE), 0)
    head0 = lane < HEAD_SIZE
    col = jnp.bitwise_and(lane, t - 1)
    strict = row > col
    incl = row >= col
    eye = jnp.where(row == col, 1.0, 0.0)

    def stack(x):
        return jnp.concatenate([jnp.where(head0, x, 0.0), jnp.where(head0, 0.0, x)], axis=0)

    def head_sums(x):
        s0 = jnp.sum(jnp.where(head0, x, 0.0), axis=-1, keepdims=True)
        s1 = jnp.sum(jnp.where(head0, 0.0, x), axis=-1, keepdims=True)
        return jnp.where(head0, s0, s1)

    rs, ks, vs, ans, bbs, cums, gs = (list(x) for x in zip(*tiles))
    pairs = range(len(tiles))

    cum_ends = [c[t - 1:t, :] for c in cums]
    decay_in = [jnp.exp(c) for c in cums]
    decay_out = [jnp.exp(-c) for c in cums]
    decay_rest = [jnp.exp(e - c) for c, e in zip(cums, cum_ends)]
    v_st = [_bf(stack(v)) for v in vs]
    ar = [jnp.concatenate([_bf(a * d), _bf(r * d)], axis=0)
          for a, r, d in zip(ans, rs, decay_in)]
    bk_st = [jnp.concatenate([_bf(stack(b * d)), _bf(stack(k * d))], axis=0)
             for b, k, d in zip(bbs, ks, decay_out)]
    bk_end = [jnp.concatenate([_bf(stack(b * d)), _bf(stack(k * d))], axis=0)
              for b, k, d in zip(bbs, ks, decay_rest)]

    scores = [_dot_nt(x, y) for x, y in zip(ar, bk_st)]
    l_ab = [jnp.where(strict, sc[:t, :LANE], 0.0) for sc in scores]
    l_ak = [jnp.where(strict, sc[:t, LANE:], 0.0) for sc in scores]
    m_rbk = [jnp.concatenate([_bf(jnp.where(incl, sc[t:, :LANE], 0.0)),
                              _bf(jnp.where(incl, sc[t:, LANE:], 0.0))], axis=1) for sc in scores]
    from_state = [_dot_nt(x, _bf(st)) for x, st in zip(ar, states)]
    rhs = [fs[:t] + _dot(_bf(l), vst) for fs, l, vst in zip(from_state, l_ak, v_st)]

    power = l_ab
    inv = [eye + l for l in l_ab]
    for _ in range(5):
        power = [_dot(_bf(x), _bf(stack(x))) for x in power]
        inv = [x + _dot(_bf(x), _bf(stack(pw))) for x, pw in zip(inv, power)]
    u = [_dot(_bf(x), _bf(stack(y))) for x, y in zip(inv, rhs)]
    uv_st = [jnp.concatenate([_bf(stack(x)), vst], axis=0) for x, vst in zip(u, v_st)]
    ys = [fs[t:] + _dot(m, x) for fs, m, x in zip(from_state, m_rbk, uv_st)]
    new_states = [st * jnp.exp(e) + _dot_tn(x, be)
                  for st, e, x, be in zip(states, cum_ends, uv_st, bk_end)]

    outs = []
    for p in pairs:
        d = ys[p] - head_sums(ys[p]) * (1.0 / HEAD_SIZE)
        var = head_sums(d * d) * (1.0 / HEAD_SIZE)
        yn = d * lax.rsqrt(var + GN_EPS)
        bonus = head_sums(rs[p] * ks[p] * r_k[p]) * vs[p]
        outs.append((yn * gn_g[p] + gn_b[p] + bonus) * gs[p])
    return outs, new_states


_PRM_NAMES = ("mu", "w0", "w_up", "a0", "a_up", "g_up", "k_k", "k_a", "pool_w", "pool_scale",
              "gn_g", "gn_b", "r_k")
_GATE_NAMES = ("v0", "v_down", "v_up")


def _timemix_kernel(*refs, has_gate):
    it = iter(refs)
    x_ref, w_ref, wout_ref = next(it), next(it), next(it)
    vf_ref = next(it) if has_gate else None
    names = _PRM_NAMES + (_GATE_NAMES if has_gate else ())
    prm = {nme: next(it) for nme in names}
    lng_ref, lnb_ref = next(it), next(it)
    xo_ref, xob_ref = next(it), next(it)
    v_ref = None if has_gate else next(it)
    z_ref, tail_ref, s_ref = next(it), next(it), next(it)

    t = CHUNK
    c = pl.program_id(0)

    @pl.when(c == 0)
    def _():
        tail_ref[...] = jnp.zeros_like(tail_ref)
        s_ref[...] = jnp.zeros_like(s_ref)

    slabs = [(s, b) for s in range(TM_CHUNKS) for b in range(BATCH)]

    def gather_rows(ref):
        return jnp.concatenate([ref[b, s * t:(s + 1) * t, :] for s, b in slabs], axis=0)

    z_ref[...] = _dot(_bf(gather_rows(x_ref)), w_ref[...])

    def pair_tile(x, b, p):
        return x[b * t:(b + 1) * t, p * LANE:(p + 1) * LANE]

    order = [(b, p) for b in range(BATCH) for p in range(N_PAIRS)]
    lanes = [slice(p * LANE, (p + 1) * LANE) for _, p in order]
    gn_g = [prm["gn_g"][:, sl] for sl in lanes]
    gn_b = [prm["gn_b"][:, sl] for sl in lanes]
    r_k = [prm["r_k"][:, sl] for sl in lanes]

    states = [s_ref[i] for i in range(len(order))]
    y_rwkv, y_pool, values = [], [], []
    for s in range(TM_CHUNKS):
        z_s = z_ref.at[s * TM_ROWS:(s + 1) * TM_ROWS]
        if s == 0:
            def history(b, r0, r1, lo, width):
                return tail_ref[b, r0:r1, lo:lo + width]
        else:
            def history(b, r0, r1, lo, width, base=(s - 1) * TM_ROWS):
                top = base + (b + 1) * t - MAX_WINDOW
                return z_ref[top + r0:top + r1, lo:lo + width]
        vf = None
        if has_gate:
            vf = jnp.concatenate([vf_ref[b, s * t:(s + 1) * t, :] for b in range(BATCH)], axis=0)
        r, k, v, an_w, bb, cum, g, pool = _prep_rows(z_s, history, c * TM_CHUNKS + s, vf, prm,
                                                     has_gate)
        tiles = [tuple(pair_tile(x, b, p) for x in (r, k, v, an_w, bb, cum, g)) for b, p in order]
        outs, states = _recurrence_chunk(tiles, states, gn_g, gn_b, r_k)
        y_rwkv += [jnp.concatenate([_bf(outs[b * N_PAIRS + p]) for p in range(N_PAIRS)], axis=1)
                   for b in range(BATCH)]
        y_pool.append(_bf(pool))
        values.append(v)

    last = (TM_CHUNKS - 1) * TM_ROWS
    for b in range(BATCH):
        tail_ref[b] = z_ref[last + (b + 1) * t - MAX_WINDOW:last + (b + 1) * t, :]

    mix = (_dot(jnp.concatenate(y_rwkv, axis=0), wout_ref[0:RWKV_WIDTH, :])
           + _dot(jnp.concatenate(y_pool, axis=0), wout_ref[RWKV_WIDTH:, :]))
    x1 = _layer_norm_rows(ALPHA * gather_rows(x_ref) + mix, lng_ref[...], lnb_ref[...])

    for i in range(len(order)):
        s_ref[i] = states[i]
    for j, (s, b) in enumerate(slabs):
        xo_ref[b, s * t:(s + 1) * t, :] = x1[j * t:(j + 1) * t, :]
        xob_ref[b, s * t:(s + 1) * t, :] = _bf(x1[j * t:(j + 1) * t, :])
        if not has_gate:
            v_ref[b, s * t:(s + 1) * t, :] = values[s][b * t:(b + 1) * t, :]


def _time_mix(x, w_in_bf, w_out_bf, vf, p, ln_g, ln_b, layer):
    has_gate = layer > 0
    t = CHUNK * TM_CHUNKS

    def rows(width):
        return pl.BlockSpec((BATCH, t, width), lambda c: (0, c, 0))

    def full(shape):
        return pl.BlockSpec(shape, lambda c: (0,) * len(shape))

    def layer_of(shape):
        return pl.BlockSpec((None,) + shape, lambda c: (layer,) + (0,) * len(shape),
                            pipeline_mode=pl.Buffered(1))

    in_specs = [rows(D_MODEL), layer_of((D_MODEL, C_USED)), layer_of((D_MODEL, D_MODEL))]
    args = [x, w_in_bf, w_out_bf]
    if has_gate:
        in_specs.append(rows(RWKV_WIDTH))
        args.append(vf)
    for nme in _PRM_NAMES + (_GATE_NAMES if has_gate else ()):
        in_specs.append(full(p[nme].shape))
        args.append(p[nme])
    in_specs += [layer_of((1, D_MODEL))] * 2
    args += [ln_g, ln_b]
    out_shape = [jax.ShapeDtypeStruct((BATCH, SEQ, D_MODEL), F32),
                 jax.ShapeDtypeStruct((BATCH, SEQ, D_MODEL), BF16)]
    out_specs = [rows(D_MODEL), rows(D_MODEL)]
    if not has_gate:
        out_shape.append(jax.ShapeDtypeStruct((BATCH, SEQ, RWKV_WIDTH), F32))
        out_specs.append(rows(RWKV_WIDTH))
    return pl.pallas_call(
        functools.partial(_timemix_kernel, has_gate=has_gate),
        grid=(SEQ // t,),
        in_specs=in_specs,
        out_specs=out_specs,
        out_shape=out_shape,
        scratch_shapes=[pltpu.VMEM((TM_CHUNKS * TM_ROWS, C_USED), F32),
                        pltpu.VMEM((BATCH, MAX_WINDOW, C_USED), F32),
                        pltpu.VMEM((BATCH * N_PAIRS, 2 * HEAD_SIZE, LANE), F32)],
        compiler_params=pltpu.CompilerParams(
            dimension_semantics=("arbitrary",), vmem_limit_bytes=VMEM_LIMIT),
        name="time_mix_gate" if has_gate else "time_mix",
    )(*args)


MLP_ROWS = 1024
MLP_FF = 512
MLP_STEPS = D_FF // MLP_FF
MLP_RES_ROWS = MLP_ROWS // MLP_STEPS


def _mlp_kernel(xb_ref, xres_ref, w1_ref, w2_ref, g_ref, b_ref, xo_ref):
    f = pl.program_id(1)

    @pl.when(f == 0)
    def _():
        xo_ref[...] = jnp.zeros_like(xo_ref)

    h = jnp.square(jnp.maximum(_dot(xb_ref[...], _bf(w1_ref[...])), 0.0))
    xo_ref[...] += _dot(_bf(h), _bf(w2_ref[...]))
    rows = pl.ds(pl.multiple_of(f * MLP_RES_ROWS, MLP_RES_ROWS), MLP_RES_ROWS)
    xo_ref[rows, :] += ALPHA * xres_ref[...]

    @pl.when(f == MLP_STEPS - 1)
    def _():
        xo_ref[...] = _layer_norm_rows(xo_ref[...], g_ref[...], b_ref[...])


def _mlp(x_bf, x, w1, w2, g, b, layer):
    m = x.shape[0]
    bm, bf = MLP_ROWS, MLP_FF
    row_spec = pl.BlockSpec((bm, D_MODEL), lambda i, f: (i, 0))
    vec_spec = pl.BlockSpec((None, 1, D_MODEL), lambda i, f: (layer, 0, 0))
    return pl.pallas_call(
        _mlp_kernel,
        grid=(m // bm, MLP_STEPS),
        in_specs=[row_spec,
                  pl.BlockSpec((MLP_RES_ROWS, D_MODEL), lambda i, f: (i * MLP_STEPS + f, 0)),
                  pl.BlockSpec((None, D_MODEL, bf), lambda i, f: (layer, 0, f)),
                  pl.BlockSpec((None, bf, D_MODEL), lambda i, f: (layer, f, 0)),
                  vec_spec, vec_spec],
        out_specs=row_spec,
        out_shape=jax.ShapeDtypeStruct((m, D_MODEL), F32),
        compiler_params=pltpu.CompilerParams(
            dimension_semantics=("arbitrary", "arbitrary"), vmem_limit_bytes=VMEM_LIMIT),
        name="mlp_ln",
    )(x_bf, x, w1, w2, g, b)


GAP_TILE = C_SHIFT // LANE
GAP_VALID = C_SHIFT - GAP_TILE * LANE


def _w_in_prep_kernel(wt_ref, o_ref):
    j = pl.program_id(1)
    tile = wt_ref[0].T
    col = lax.broadcasted_iota(jnp.int32, tile.shape, 1)
    tile = jnp.where((j == GAP_TILE) & (col >= GAP_VALID), 0.0, tile)
    o_ref[...] = _bf(tile)


def _w_in_prep(w_in):
    w_t = jnp.swapaxes(w_in, 1, 2)

    def source_row(j):
        shift = jnp.where(j > GAP_TILE, C_POOL - C_SHIFT, 0)
        return pl.multiple_of(j * LANE - shift, math.gcd(LANE, C_POOL - C_SHIFT))

    return pl.pallas_call(
        _w_in_prep_kernel,
        grid=(DEPTH, C_USED // LANE),
        in_specs=[pl.BlockSpec((pl.Element(1), pl.Element(LANE), pl.Element(D_MODEL)),
                               lambda l, j: (l, source_row(j), 0))],
        out_specs=pl.BlockSpec((None, D_MODEL, LANE), lambda l, j: (l, 0, j)),
        out_shape=jax.ShapeDtypeStruct((DEPTH, D_MODEL, C_USED), BF16),
        compiler_params=pltpu.CompilerParams(
            dimension_semantics=("arbitrary", "arbitrary"), vmem_limit_bytes=VMEM_LIMIT),
        name="w_in_prep",
    )(w_t)


def _pad_rows(w, rows, at=0):
    out = jnp.zeros((rows, w.shape[1]), w.dtype)
    return out.at[at:at + w.shape[0]].set(w)


def _layer_params(l, mu, w0, w_up, a0, a_up, g_up, v0, v_down, v_up, k_k, k_a, r_k,
                  gn_g, gn_b, pool_w, pool_scale):
    p = {
        "mu": jnp.concatenate([mu[l], jnp.zeros((C_USED - C_SHIFT,), F32)])[None, :],
        "w0": w0[l][None, :],
        "w_up": _bf(_pad_rows(w_up[l], LANE, 0)),
        "a0": a0[l][None, :],
        "a_up": _bf(_pad_rows(a_up[l], LANE, R_DECAY)),
        "g_up": _bf(_pad_rows(g_up[l], 2 * LANE, 0)),
        "k_k": k_k[l][None, :],
        "k_a": k_a[l][None, :],
        "pool_w": _bf(pool_w[l]),
        "pool_scale": pool_scale[l][None, :],
        "r_k": r_k[l].reshape(1, RWKV_WIDTH),
        "gn_g": gn_g[l][None, :],
        "gn_b": gn_b[l][None, :],
    }
    if l > 0:
        p["v0"] = v0[l - 1][None, :]
        p["v_down"] = _bf(jnp.pad(v_down[l - 1], ((0, 0), (0, LANE - R_MV))))
        p["v_up"] = _bf(_pad_rows(v_up[l - 1], LANE, 0))
    return p


def kernel(x, w_in, mu, w0, w_up, a0, a_up, g_up, v0, v_down, v_up, k_k, k_a, r_k, gn_g, gn_b,
           pool_w, pool_scale, w_out, ln1_g, ln1_b, mlp_w1, mlp_w2, ln2_g, ln2_b):
    m = BATCH * SEQ
    w_in_bf = _w_in_prep(w_in)
    w_out_bf = _bf(w_out)
    ln1_g, ln1_b, ln2_g, ln2_b = (v.reshape(DEPTH, 1, D_MODEL) for v in (ln1_g, ln1_b, ln2_g, ln2_b))
    v_first = None
    for l in range(DEPTH):
        p = _layer_params(l, mu, w0, w_up, a0, a_up, g_up, v0, v_down, v_up, k_k, k_a,
                          r_k, gn_g, gn_b, pool_w, pool_scale)
        res = _time_mix(x, w_in_bf, w_out_bf, v_first, p, ln1_g, ln1_b, l)
        x, x_bf = res[0], res[1]
        if l == 0:
            v_first = res[2]
        x = _mlp(x_bf.reshape(m, D_MODEL), x.reshape(m, D_MODEL), mlp_w1, mlp_w2,
                 ln2_g, ln2_b, l).reshape(BATCH, SEQ, D_MODEL)
    return x
```

```python
import functools

import jax
import jax.numpy as jnp
from jax import lax
from jax.experimental import pallas as pl
from jax.experimental.pallas import tpu as pltpu

D_MODEL = 2048
BATCH = 2
SEQ = 8192
DEPTH = 2
RWKV_WIDTH = 1024
POOL_WIDTH = 1024
HEAD_SIZE = 64
POOL_WINDOWS = (2, 4, 8, 16)
POOL_GROUP = 256
R_DECAY = 64
R_AAA = 64
R_MV = 32
R_GATE = 160
C_SHIFT = 3 * RWKV_WIDTH + R_DECAY + R_AAA + R_GATE
C_IN = C_SHIFT + POOL_WIDTH
D_FF = 4 * D_MODEL
ALPHA = (2.0 * DEPTH) ** 0.25
LN_EPS = 1e-5
GN_EPS = 64e-5
L2_EPS = 1e-12

LANE = 128
CHUNK = 64
N_PAIRS = RWKV_WIDTH // LANE
MAX_WINDOW = max(POOL_WINDOWS)

C_LR = 3 * RWKV_WIDTH
C_LR_W = 3 * LANE
C_POOL = C_LR + C_LR_W
C_USED = C_POOL + POOL_WIDTH

VMEM_LIMIT = 58 * 1024 * 1024

F32 = jnp.float32
BF16 = jnp.bfloat16
F32_PIECES = 3


def _bf(x):
    return x.astype(BF16)


def _dot(a, b):
    return jnp.dot(a, b, preferred_element_type=F32)


def _dot_nt(a, b):
    return lax.dot_general(a, b, (((1,), (1,)), ((), ())), preferred_element_type=F32)


def _dot_tn(a, b):
    return lax.dot_general(a, b, (((0,), (0,)), ((), ())), preferred_element_type=F32)


def _split_dot(x, w_bf, n_terms):
    acc = None
    rem = x
    for _ in range(n_terms):
        piece = _bf(rem)
        term = _dot(piece, w_bf)
        acc = term if acc is None else acc + term
        rem = rem - piece.astype(F32)
    return acc


def _split_dot_left(w_bf, x, n_terms):
    acc = None
    rem = x
    for _ in range(n_terms):
        piece = _bf(rem)
        term = _dot(w_bf, piece)
        acc = term if acc is None else acc + term
        rem = rem - piece.astype(F32)
    return acc


def _sigmoid(x):
    return 1.0 / (1.0 + jnp.exp(-x))


def _layer_norm_rows(h, g, b):
    mean = jnp.mean(h, axis=-1, keepdims=True)
    d = h - mean
    var = jnp.mean(d * d, axis=-1, keepdims=True)
    return d * lax.rsqrt(var + LN_EPS) * g + b


TM_ROWS = BATCH * CHUNK
TM_CHUNKS = 2


def _prep_rows(z_ref, history, chunk, vf, prm, has_gate):
    t = CHUNK

    def token_shift_mix(lo, width):
        cur = z_ref[:, lo:lo + width]
        row = lax.broadcasted_iota(jnp.int32, cur.shape, 0)
        prev = pltpu.roll(cur, 1, axis=0)
        for b in range(BATCH):
            last = history(b, MAX_WINDOW - 1, MAX_WINDOW, lo, width)
            prev = jnp.where(row == b * t, last, prev)
        return cur + (prev - cur) * prm["mu"][:, lo:lo + width]

    zl = token_shift_mix(C_LR, C_LR_W)
    xwa = zl[:, 0:LANE]
    xg = zl[:, LANE:C_LR_W]
    w_pre = prm["w0"][...] + _dot(_bf(jnp.tanh(xwa)), prm["w_up"][...])
    lw = -jnp.exp(-0.5) * _sigmoid(w_pre)
    a = _sigmoid(prm["a0"][...] + _dot(_bf(xwa), prm["a_up"][...]))
    g = _dot(_bf(_sigmoid(xg)), prm["g_up"][...])

    rr = lax.broadcasted_iota(jnp.int32, (TM_ROWS, TM_ROWS), 0)
    cc = lax.broadcasted_iota(jnp.int32, (TM_ROWS, TM_ROWS), 1)
    chunk_bits = t.bit_length() - 1
    same_chunk = jnp.right_shift(rr, chunk_bits) == jnp.right_shift(cc, chunk_bits)
    tri = jnp.where((rr >= cc) & same_chunk, 1.0, 0.0).astype(BF16)
    cum = _split_dot_left(tri, lw, F32_PIECES)

    r = token_shift_mix(0, RWKV_WIDTH)
    k = token_shift_mix(RWKV_WIDTH, RWKV_WIDTH)
    v = token_shift_mix(2 * RWKV_WIDTH, RWKV_WIDTH)
    if has_gate:
        low = _dot(_bf(v), prm["v_down"][...])
        gate = _sigmoid(prm["v0"][...] + _dot(_bf(low), prm["v_up"][...]))
        v = v + (vf - v) * gate

    kk = k * prm["k_k"][...]
    sr = lax.broadcasted_iota(jnp.int32, (2 * LANE, 2 * LANE), 0)
    sc = lax.broadcasted_iota(jnp.int32, (2 * LANE, 2 * LANE), 1)
    head_bits = HEAD_SIZE.bit_length() - 1
    seg = jnp.where(jnp.right_shift(sr, head_bits) == jnp.right_shift(sc, head_bits),
                    1.0, 0.0).astype(BF16)
    sq = kk * kk
    ss = jnp.concatenate(
        [_split_dot(sq[:, j:j + 2 * LANE], seg, F32_PIECES)
         for j in range(0, RWKV_WIDTH, 2 * LANE)], axis=1)
    kkn = kk / jnp.maximum(jnp.sqrt(ss), L2_EPS)
    an_w = -kkn * jnp.exp(-lw)
    bb = kkn * a
    k_mod = k * (1.0 + (a - 1.0) * prm["k_a"][...])

    pos = chunk * t + lax.broadcasted_iota(jnp.int32, (t, POOL_GROUP), 0) + 1
    outs = []
    for gi, win in enumerate(POOL_WINDOWS):
        lo = C_POOL + gi * POOL_GROUP
        cnt = jnp.minimum(pos, win).astype(F32)
        d_rows = []
        for b in range(BATCH):
            u = z_ref[b * t:(b + 1) * t, lo:lo + POOL_GROUP]
            e = jnp.concatenate([history(b, 0, MAX_WINDOW, lo, POOL_GROUP), u], axis=0)
            step = 1
            while step < win:
                e = e + pltpu.roll(e, step, axis=0)
                step *= 2
            d_rows.append(e[MAX_WINDOW:MAX_WINDOW + t, :] / cnt - u)
        outs.append(_dot(_bf(jnp.concatenate(d_rows, axis=0)), prm["pool_w"][gi]))
    y_pool = jnp.concatenate(outs, axis=1) * prm["pool_scale"][...]
    return r, k_mod, v, an_w, bb, cum, g, y_pool


def _recurrence_chunk(tiles, states, gn_g, gn_b, r_k):
    t = CHUNK
    lane = lax.broadcasted_iota(jnp.int32, (t, LANE), 1)
    row = lax.broadcasted_iota(jnp.int32, (t, LANE), 0)
    head0 = lane < HEAD_SIZE
    col = jnp.bitwise_and(lane, t - 1)
    strict = row > col
    incl = row >= col
    eye = jnp.where(row == col, 1.0, 0.0)

    def stack(x):
        return jnp.concatenate([jnp.where(head0, x, 0.0), jnp.where(head0, 0.0, x)], axis=0)

    def head_sums(x):
        s0 = jnp.sum(jnp.where(head0, x, 0.0), axis=-1, keepdims=True)
        s1 = jnp.sum(jnp.where(head0, 0.0, x), axis=-1, keepdims=True)
        return jnp.where(head0, s0, s1)

    rs, ks, vs, ans, bbs, cums, gs = (list(x) for x in zip(*tiles))
    pairs = range(len(tiles))

    cum_ends = [c[t - 1:t, :] for c in cums]
    decay_in = [jnp.exp(c) for c in cums]
    decay_out = [jnp.exp(-c) for c in cums]
    decay_rest = [jnp.exp(e - c) for c, e in zip(cums, cum_ends)]
    v_st = [_bf(stack(v)) for v in vs]
    ar = [jnp.concatenate([_bf(a * d), _bf(r * d)], axis=0)
          for a, r, d in zip(ans, rs, decay_in)]
    bk_st = [jnp.concatenate([_bf(stack(b * d)), _bf(stack(k * d))], axis=0)
             for b, k, d in zip(bbs, ks, decay_out)]
    bk_end = [jnp.concatenate([_bf(stack(b * d)), _bf(stack(k * d))], axis=0)
              for b, k, d in zip(bbs, ks, decay_rest)]

    scores = [_dot_nt(x, y) for x, y in zip(ar, bk_st)]
    l_ab = [jnp.where(strict, sc[:t, :LANE], 0.0) for sc in scores]
    l_ak = [jnp.where(strict, sc[:t, LANE:], 0.0) for sc in scores]
    m_rbk = [jnp.concatenate([_bf(jnp.where(incl, sc[t:, :LANE], 0.0)),
                              _bf(jnp.where(incl, sc[t:, LANE:], 0.0))], axis=1) for sc in scores]
    from_state = [_dot_nt(x, _bf(st)) for x, st in zip(ar, states)]
    rhs = [fs[:t] + _dot(_bf(l), vst) for fs, l, vst in zip(from_state, l_ak, v_st)]

    power = l_ab
    inv = [eye + l for l in l_ab]
    for _ in range(5):
        power = [_dot(_bf(x), _bf(stack(x))) for x in power]
        inv = [x + _dot(_bf(x), _bf(stack(pw))) for x, pw in zip(inv, power)]
    u = [_dot(_bf(x), _bf(stack(y))) for x, y in zip(inv, rhs)]
    uv_st = [jnp.concatenate([_bf(stack(x)), vst], axis=0) for x, vst in zip(u, v_st)]
    ys = [fs[t:] + _dot(m, x) for fs, m, x in zip(from_state, m_rbk, uv_st)]
    new_states = [st * jnp.exp(e) + _dot_tn(x, be)
                  for st, e, x, be in zip(states, cum_ends, uv_st, bk_end)]

    outs = []
    for p in pairs:
        d = ys[p] - head_sums(ys[p]) * (1.0 / HEAD_SIZE)
        var = head_sums(d * d) * (1.0 / HEAD_SIZE)
        yn = d * lax.rsqrt(var + GN_EPS)
        bonus = head_sums(rs[p] * ks[p] * r_k[p]) * vs[p]
        outs.append((yn * gn_g[p] + gn_b[p] + bonus) * gs[p])
    return outs, new_states


_PRM_NAMES = ("mu", "w0", "w_up", "a0", "a_up", "g_up", "k_k", "k_a", "pool_w", "pool_scale",
              "gn_g", "gn_b", "r_k")
_GATE_NAMES = ("v0", "v_down", "v_up")


def _timemix_kernel(*refs, has_gate):
    it = iter(refs)
    x_ref, w_ref, wout_ref = next(it), next(it), next(it)
    vf_ref = next(it) if has_gate else None
    names = _PRM_NAMES + (_GATE_NAMES if has_gate else ())
    prm = {nme: next(it) for nme in names}
    lng_ref, lnb_ref = next(it), next(it)
    xo_ref, xob_ref = next(it), next(it)
    v_ref = None if has_gate else next(it)
    z_ref, tail_ref, s_ref = next(it), next(it), next(it)

    t = CHUNK
    c = pl.program_id(0)

    @pl.when(c == 0)
    def _():
        tail_ref[...] = jnp.zeros_like(tail_ref)
        s_ref[...] = jnp.zeros_like(s_ref)

    slabs = [(s, b) for s in range(TM_CHUNKS) for b in range(BATCH)]

    def gather_rows(ref):
        return jnp.concatenate([ref[b, s * t:(s + 1) * t, :] for s, b in slabs], axis=0)

    z_ref[...] = _dot(_bf(gather_rows(x_ref)), w_ref[...])

    def pair_tile(x, b, p):
        return x[b * t:(b + 1) * t, p * LANE:(p + 1) * LANE]

    order = [(b, p) for b in range(BATCH) for p in range(N_PAIRS)]
    lanes = [slice(p * LANE, (p + 1) * LANE) for _, p in order]
    gn_g = [prm["gn_g"][:, sl] for sl in lanes]
    gn_b = [prm["gn_b"][:, sl] for sl in lanes]
    r_k = [prm["r_k"][:, sl] for sl in lanes]

    states = [s_ref[i] for i in range(len(order))]
    y_rwkv, y_pool, values = [], [], []
    for s in range(TM_CHUNKS):
        z_s = z_ref.at[s * TM_ROWS:(s + 1) * TM_ROWS]
        if s == 0:
            def history(b, r0, r1, lo, width):
                return tail_ref[b, r0:r1, lo:lo + width]
        else:
            def history(b, r0, r1, lo, width, base=(s - 1) * TM_ROWS):
                top = base + (b + 1) * t - MAX_WINDOW
                return z_ref[top + r0:top + r1, lo:lo + width]
        vf = None
        if has_gate:
            vf = jnp.concatenate([vf_ref[b, s * t:(s + 1) * t, :] for b in range(BATCH)], axis=0)
        r, k, v, an_w, bb, cum, g, pool = _prep_rows(z_s, history, c * TM_CHUNKS + s, vf, prm,
                                                     has_gate)
        tiles = [tuple(pair_tile(x, b, p) for x in (r, k, v, an_w, bb, cum, g)) for b, p in order]
        outs, states = _recurrence_chunk(tiles, states, gn_g, gn_b, r_k)
        y_rwkv += [jnp.concatenate([_bf(outs[b * N_PAIRS + p]) for p in range(N_PAIRS)], axis=1)
                   for b in range(BATCH)]
        y_pool.append(_bf(pool))
        values.append(v)

    last = (TM_CHUNKS - 1) * TM_ROWS
    for b in range(BATCH):
        tail_ref[b] = z_ref[last + (b + 1) * t - MAX_WINDOW:last + (b + 1) * t, :]

    mix = (_dot(jnp.concatenate(y_rwkv, axis=0), wout_ref[0:RWKV_WIDTH, :])
           + _dot(jnp.concatenate(y_pool, axis=0), wout_ref[RWKV_WIDTH:, :]))
    x1 = _layer_norm_rows(ALPHA * gather_rows(x_ref) + mix, lng_ref[...], lnb_ref[...])

    for i in range(len(order)):
        s_ref[i] = states[i]
    for j, (s, b) in enumerate(slabs):
        xo_ref[b, s * t:(s + 1) * t, :] = x1[j * t:(j + 1) * t, :]
        xob_ref[b, s * t:(s + 1) * t, :] = _bf(x1[j * t:(j + 1) * t, :])
        if not has_gate:
            v_ref[b, s * t:(s + 1) * t, :] = values[s][b * t:(b + 1) * t, :]


def _time_mix(x, w_in_bf, w_out_bf, vf, p, ln_g, ln_b, layer):
    has_gate = layer > 0
    t = CHUNK * TM_CHUNKS

    def rows(width):
        return pl.BlockSpec((BATCH, t, width), lambda c: (0, c, 0))

    def full(shape):
        return pl.BlockSpec(shape, lambda c: (0,) * len(shape))

    def layer_of(shape):
        return pl.BlockSpec((None,) + shape, lambda c: (layer,) + (0,) * len(shape),
                            pipeline_mode=pl.Buffered(1))

    in_specs = [rows(D_MODEL), layer_of((D_MODEL, C_USED)), layer_of((D_MODEL, D_MODEL))]
    args = [x, w_in_bf, w_out_bf]
    if has_gate:
        in_specs.append(rows(RWKV_WIDTH))
        args.append(vf)
    for nme in _PRM_NAMES + (_GATE_NAMES if has_gate else ()):
        in_specs.append(full(p[nme].shape))
        args.append(p[nme])
    in_specs += [layer_of((1, D_MODEL))] * 2
    args += [ln_g, ln_b]
    out_shape = [jax.ShapeDtypeStruct((BATCH, SEQ, D_MODEL), F32),
                 jax.ShapeDtypeStruct((BATCH, SEQ, D_MODEL), BF16)]
    out_specs = [rows(D_MODEL), rows(D_MODEL)]
    if not has_gate:
        out_shape.append(jax.ShapeDtypeStruct((BATCH, SEQ, RWKV_WIDTH), F32))
        out_specs.append(rows(RWKV_WIDTH))
    return pl.pallas_call(
        functools.partial(_timemix_kernel, has_gate=has_gate),
        grid=(SEQ // t,),
        in_specs=in_specs,
        out_specs=out_specs,
        out_shape=out_shape,
        scratch_shapes=[pltpu.VMEM((TM_CHUNKS * TM_ROWS, C_USED), F32),
                        pltpu.VMEM((BATCH, MAX_WINDOW, C_USED), F32),
                        pltpu.VMEM((BATCH * N_PAIRS, 2 * HEAD_SIZE, LANE), F32)],
        compiler_params=pltpu.CompilerParams(
            dimension_semantics=("arbitrary",), vmem_limit_bytes=VMEM_LIMIT),
        name="time_mix_gate" if has_gate else "time_mix",
    )(*args)


MLP_ROWS = 1024
MLP_FF = 512
MLP_STEPS = D_FF // MLP_FF
MLP_RES_ROWS = MLP_ROWS // MLP_STEPS


def _mlp_kernel(xb_ref, xres_ref, w1_ref, w2_ref, g_ref, b_ref, xo_ref):
    f = pl.program_id(1)

    @pl.when(f == 0)
    def _():
        xo_ref[...] = jnp.zeros_like(xo_ref)

    h = jnp.square(jnp.maximum(_dot(xb_ref[...], _bf(w1_ref[...])), 0.0))
    xo_ref[...] += _dot(_bf(h), _bf(w2_ref[...]))
    rows = pl.ds(pl.multiple_of(f * MLP_RES_ROWS, MLP_RES_ROWS), MLP_RES_ROWS)
    xo_ref[rows, :] += ALPHA * xres_ref[...]

    @pl.when(f == MLP_STEPS - 1)
    def _():
        xo_ref[...] = _layer_norm_rows(xo_ref[...], g_ref[...], b_ref[...])


def _mlp(x_bf, x, w1, w2, g, b, layer):
    m = x.shape[0]
    bm, bf = MLP_ROWS, MLP_FF
    row_spec = pl.BlockSpec((bm, D_MODEL), lambda i, f: (i, 0))
    vec_spec = pl.BlockSpec((None, 1, D_MODEL), lambda i, f: (layer, 0, 0))
    return pl.pallas_call(
        _mlp_kernel,
        grid=(m // bm, MLP_STEPS),
        in_specs=[row_spec,
                  pl.BlockSpec((MLP_RES_ROWS, D_MODEL), lambda i, f: (i * MLP_STEPS + f, 0)),
                  pl.BlockSpec((None, D_MODEL, bf), lambda i, f: (layer, 0, f)),
                  pl.BlockSpec((None, bf, D_MODEL), lambda i, f: (layer, f, 0)),
                  vec_spec, vec_spec],
        out_specs=row_spec,
        out_shape=jax.ShapeDtypeStruct((m, D_MODEL), F32),
        compiler_params=pltpu.CompilerParams(
            dimension_semantics=("arbitrary", "arbitrary"), vmem_limit_bytes=VMEM_LIMIT),
        name="mlp_ln",
    )(x_bf, x, w1, w2, g, b)


W_PREP_ROWS = 256


def _w_in_prep_kernel(wt_ref, o_ref):
    gap = jnp.zeros((C_POOL - C_SHIFT, W_PREP_ROWS), F32)
    padded = jnp.concatenate([wt_ref[0:C_SHIFT, :], gap, wt_ref[C_SHIFT:C_IN, :]], axis=0)
    o_ref[...] = _bf(padded.T)


def _w_in_prep(w_in):
    w_t = jnp.swapaxes(w_in, 1, 2)
    return pl.pallas_call(
        _w_in_prep_kernel,
        grid=(DEPTH, D_MODEL // W_PREP_ROWS),
        in_specs=[pl.BlockSpec((None, C_IN, W_PREP_ROWS), lambda l, i: (l, 0, i))],
        out_specs=pl.BlockSpec((None, W_PREP_ROWS, C_USED), lambda l, i: (l, i, 0)),
        out_shape=jax.ShapeDtypeStruct((DEPTH, D_MODEL, C_USED), BF16),
        compiler_params=pltpu.CompilerParams(
            dimension_semantics=("arbitrary", "arbitrary"), vmem_limit_bytes=VMEM_LIMIT),
        name="w_in_prep",
    )(w_t)


def _pad_rows(w, rows, at=0):
    out = jnp.zeros((rows, w.shape[1]), w.dtype)
    return out.at[at:at + w.shape[0]].set(w)


def _layer_params(l, mu, w0, w_up, a0, a_up, g_up, v0, v_down, v_up, k_k, k_a, r_k,
                  gn_g, gn_b, pool_w, pool_scale):
    p = {
        "mu": jnp.concatenate([mu[l], jnp.zeros((C_USED - C_SHIFT,), F32)])[None, :],
        "w0": w0[l][None, :],
        "w_up": _bf(_pad_rows(w_up[l], LANE, 0)),
        "a0": a0[l][None, :],
        "a_up": _bf(_pad_rows(a_up[l], LANE, R_DECAY)),
        "g_up": _bf(_pad_rows(g_up[l], 2 * LANE, 0)),
        "k_k": k_k[l][None, :],
        "k_a": k_a[l][None, :],
        "pool_w": _bf(pool_w[l]),
        "pool_scale": pool_scale[l][None, :],
        "r_k": r_k[l].reshape(1, RWKV_WIDTH),
        "gn_g": gn_g[l][None, :],
        "gn_b": gn_b[l][None, :],
    }
    if l > 0:
        p["v0"] = v0[l - 1][None, :]
        p["v_down"] = _bf(jnp.pad(v_down[l - 1], ((0, 0), (0, LANE - R_MV))))
        p["v_up"] = _bf(_pad_rows(v_up[l - 1], LANE, 0))
    return p


def kernel(x, w_in, mu, w0, w_up, a0, a_up, g_up, v0, v_down, v_up, k_k, k_a, r_k, gn_g, gn_b,
           pool_w, pool_scale, w_out, ln1_g, ln1_b, mlp_w1, mlp_w2, ln2_g, ln2_b):
    m = BATCH * SEQ
    w_in_bf = _w_in_prep(w_in)
    w_out_bf = _bf(w_out)
    ln1_g, ln1_b, ln2_g, ln2_b = (v.reshape(DEPTH, 1, D_MODEL) for v in (ln1_g, ln1_b, ln2_g, ln2_b))
    v_first = None
    for l in range(DEPTH):
        p = _layer_params(l, mu, w0, w_up, a0, a_up, g_up, v0, v_down, v_up, k_k, k_a,
                          r_k, gn_g, gn_b, pool_w, pool_scale)
        res = _time_mix(x, w_in_bf, w_out_bf, v_first, p, ln1_g, ln1_b, l)
        x, x_bf = res[0], res[1]
        if l == 0:
            v_first = res[2]
        x = _mlp(x_bf.reshape(m, D_MODEL), x.reshape(m, D_MODEL), mlp_w1, mlp_w2,
                 ln2_g, ln2_b, l).reshape(BATCH, SEQ, D_MODEL)
    return x
```

```python
import functools

import jax
import jax.numpy as jnp
from jax import lax
from jax.experimental import pallas as pl
from jax.experimental.pallas import tpu as pltpu

D_MODEL = 2048
BATCH = 2
SEQ = 8192
DEPTH = 2
RWKV_WIDTH = 1024
POOL_WIDTH = 1024
HEAD_SIZE = 64
POOL_WINDOWS = (2, 4, 8, 16)
POOL_GROUP = 256
R_DECAY = 64
R_AAA = 64
R_MV = 32
R_GATE = 160
C_SHIFT = 3 * RWKV_WIDTH + R_DECAY + R_AAA + R_GATE
C_IN = C_SHIFT + POOL_WIDTH
D_FF = 4 * D_MODEL
ALPHA = (2.0 * DEPTH) ** 0.25
LN_EPS = 1e-5
GN_EPS = 64e-5
L2_EPS = 1e-12

LANE = 128
CHUNK = 64
N_PAIRS = RWKV_WIDTH // LANE
MAX_WINDOW = max(POOL_WINDOWS)

C_LR = 3 * RWKV_WIDTH
C_LR_W = 3 * LANE
C_POOL = C_LR + C_LR_W
C_USED = C_POOL + POOL_WIDTH

VMEM_LIMIT = 58 * 1024 * 1024

F32 = jnp.float32
BF16 = jnp.bfloat16
F32_PIECES = 3


def _bf(x):
    return x.astype(BF16)


def _dot(a, b):
    return jnp.dot(a, b, preferred_element_type=F32)


def _dot_nt(a, b):
    return lax.dot_general(a, b, (((1,), (1,)), ((), ())), preferred_element_type=F32)


def _dot_tn(a, b):
    return lax.dot_general(a, b, (((0,), (0,)), ((), ())), preferred_element_type=F32)


def _split_dot(x, w_bf, n_terms):
    acc = None
    rem = x
    for _ in range(n_terms):
        piece = _bf(rem)
        term = _dot(piece, w_bf)
        acc = term if acc is None else acc + term
        rem = rem - piece.astype(F32)
    return acc


def _split_dot_left(w_bf, x, n_terms):
    acc = None
    rem = x
    for _ in range(n_terms):
        piece = _bf(rem)
        term = _dot(w_bf, piece)
        acc = term if acc is None else acc + term
        rem = rem - piece.astype(F32)
    return acc


def _sigmoid(x):
    return 1.0 / (1.0 + jnp.exp(-x))


def _layer_norm_rows(h, g, b):
    mean = jnp.mean(h, axis=-1, keepdims=True)
    d = h - mean
    var = jnp.mean(d * d, axis=-1, keepdims=True)
    return d * lax.rsqrt(var + LN_EPS) * g + b


TM_ROWS = BATCH * CHUNK
TM_CHUNKS = 2


def _prep_rows(z_ref, history, chunk, vf, prm, has_gate):
    t = CHUNK

    def token_shift_mix(lo, width):
        cur = z_ref[:, lo:lo + width]
        row = lax.broadcasted_iota(jnp.int32, cur.shape, 0)
        prev = pltpu.roll(cur, 1, axis=0)
        for b in range(BATCH):
            last = history(b, MAX_WINDOW - 1, MAX_WINDOW, lo, width)
            prev = jnp.where(row == b * t, last, prev)
        return cur + (prev - cur) * prm["mu"][:, lo:lo + width]

    zl = token_shift_mix(C_LR, C_LR_W)
    xwa = zl[:, 0:LANE]
    xg = zl[:, LANE:C_LR_W]
    w_pre = prm["w0"][...] + _dot(_bf(jnp.tanh(xwa)), prm["w_up"][...])
    lw = -jnp.exp(-0.5) * _sigmoid(w_pre)
    a = _sigmoid(prm["a0"][...] + _dot(_bf(xwa), prm["a_up"][...]))
    g = _dot(_bf(_sigmoid(xg)), prm["g_up"][...])

    rr = lax.broadcasted_iota(jnp.int32, (TM_ROWS, TM_ROWS), 0)
    cc = lax.broadcasted_iota(jnp.int32, (TM_ROWS, TM_ROWS), 1)
    chunk_bits = t.bit_length() - 1
    same_chunk = jnp.right_shift(rr, chunk_bits) == jnp.right_shift(cc, chunk_bits)
    tri = jnp.where((rr >= cc) & same_chunk, 1.0, 0.0).astype(BF16)
    cum = _split_dot_left(tri, lw, F32_PIECES)

    r = token_shift_mix(0, RWKV_WIDTH)
    k = token_shift_mix(RWKV_WIDTH, RWKV_WIDTH)
    v = token_shift_mix(2 * RWKV_WIDTH, RWKV_WIDTH)
    if has_gate:
        low = _dot(_bf(v), prm["v_down"][...])
        gate = _sigmoid(prm["v0"][...] + _dot(_bf(low), prm["v_up"][...]))
        v = v + (vf - v) * gate

    kk = k * prm["k_k"][...]
    sr = lax.broadcasted_iota(jnp.int32, (2 * LANE, 2 * LANE), 0)
    sc = lax.broadcasted_iota(jnp.int32, (2 * LANE, 2 * LANE), 1)
    head_bits = HEAD_SIZE.bit_length() - 1
    seg = jnp.where(jnp.right_shift(sr, head_bits) == jnp.right_shift(sc, head_bits),
                    1.0, 0.0).astype(BF16)
    sq = kk * kk
    ss = jnp.concatenate(
        [_split_dot(sq[:, j:j + 2 * LANE], seg, F32_PIECES)
         for j in range(0, RWKV_WIDTH, 2 * LANE)], axis=1)
    kkn = kk / jnp.maximum(jnp.sqrt(ss), L2_EPS)
    an_w = -kkn * jnp.exp(-lw)
    bb = kkn * a
    k_mod = k * (1.0 + (a - 1.0) * prm["k_a"][...])

    pos = chunk * t + lax.broadcasted_iota(jnp.int32, (t, POOL_GROUP), 0) + 1
    outs = []
    for gi, win in enumerate(POOL_WINDOWS):
        lo = C_POOL + gi * POOL_GROUP
        cnt = jnp.minimum(pos, win).astype(F32)
        d_rows = []
        for b in range(BATCH):
            u = z_ref[b * t:(b + 1) * t, lo:lo + POOL_GROUP]
            e = jnp.concatenate([history(b, 0, MAX_WINDOW, lo, POOL_GROUP), u], axis=0)
            step = 1
            while step < win:
                e = e + pltpu.roll(e, step, axis=0)
                step *= 2
            d_rows.append(e[MAX_WINDOW:MAX_WINDOW + t, :] / cnt - u)
        outs.append(_dot(_bf(jnp.concatenate(d_rows, axis=0)), prm["pool_w"][gi]))
    y_pool = jnp.concatenate(outs, axis=1) * prm["pool_scale"][...]
    return r, k_mod, v, an_w, bb, cum, g, y_pool


def _recurrence_chunk(tiles, states, gn_g, gn_b, r_k):
    t = CHUNK
    lane = lax.broadcasted_iota(jnp.int32, (t, LANE), 1)
    row = lax.broadcasted_iota(jnp.int32, (t, LANE), 0)
    head0 = lane < HEAD_SIZE
    col = jnp.bitwise_and(lane, t - 1)
    strict = row > col
    incl = row >= col
    eye = jnp.where(row == col, 1.0, 0.0)

    def stack(x):
        return jnp.concatenate([jnp.where(head0, x, 0.0), jnp.where(head0, 0.0, x)], axis=0)

    def head_sums(x):
        s0 = jnp.sum(jnp.where(head0, x, 0.0), axis=-1, keepdims=True)
        s1 = jnp.sum(jnp.where(head0, 0.0, x), axis=-1, keepdims=True)
        return jnp.where(head0, s0, s1)

    rs, ks, vs, ans, bbs, cums, gs = (list(x) for x in zip(*tiles))
    pairs = range(len(tiles))

    cum_ends = [c[t - 1:t, :] for c in cums]
    decay_in = [jnp.exp(c) for c in cums]
    decay_out = [jnp.exp(-c) for c in cums]
    decay_rest = [jnp.exp(e - c) for c, e in zip(cums, cum_ends)]
    v_st = [_bf(stack(v)) for v in vs]
    ar = [jnp.concatenate([_bf(a * d), _bf(r * d)], axis=0)
          for a, r, d in zip(ans, rs, decay_in)]
    bk_st = [jnp.concatenate([_bf(stack(b * d)), _bf(stack(k * d))], axis=0)
             for b, k, d in zip(bbs, ks, decay_out)]
    bk_end = [jnp.concatenate([_bf(stack(b * d)), _bf(stack(k * d))], axis=0)
              for b, k, d in zip(bbs, ks, decay_rest)]

    scores = [_dot_nt(x, y) for x, y in zip(ar, bk_st)]
    l_ab = [jnp.where(strict, sc[:t, :LANE], 0.0) for sc in scores]
    l_ak = [jnp.where(strict, sc[:t, LANE:], 0.0) for sc in scores]
    m_rbk = [jnp.concatenate([_bf(jnp.where(incl, sc[t:, :LANE], 0.0)),
                              _bf(jnp.where(incl, sc[t:, LANE:], 0.0))], axis=1) for sc in scores]
    from_state = [_dot_nt(x, _bf(st)) for x, st in zip(ar, states)]
    rhs = [fs[:t] + _dot(_bf(l), vst) for fs, l, vst in zip(from_state, l_ak, v_st)]

    power = l_ab
    inv = [eye + l for l in l_ab]
    for _ in range(5):
        power = [_dot(_bf(x), _bf(stack(x))) for x in power]
        inv = [x + _dot(_bf(x), _bf(stack(pw))) for x, pw in zip(inv, power)]
    u = [_dot(_bf(x), _bf(stack(y))) for x, y in zip(inv, rhs)]
    uv_st = [jnp.concatenate([_bf(stack(x)), vst], axis=0) for x, vst in zip(u, v_st)]
    ys = [fs[t:] + _dot(m, x) for fs, m, x in zip(from_state, m_rbk, uv_st)]
    new_states = [st * jnp.exp(e) + _dot_tn(x, be)
                  for st, e, x, be in zip(states, cum_ends, uv_st, bk_end)]

    outs = []
    for p in pairs:
        d = ys[p] - head_sums(ys[p]) * (1.0 / HEAD_SIZE)
        var = head_sums(d * d) * (1.0 / HEAD_SIZE)
        yn = d * lax.rsqrt(var + GN_EPS)
        bonus = head_sums(rs[p] * ks[p] * r_k[p]) * vs[p]
        outs.append((yn * gn_g[p] + gn_b[p] + bonus) * gs[p])
    return outs, new_states


_PRM_NAMES = ("mu", "w0", "w_up", "a0", "a_up", "g_up", "k_k", "k_a", "pool_w", "pool_scale",
              "gn_g", "gn_b", "r_k")
_GATE_NAMES = ("v0", "v_down", "v_up")


def _timemix_kernel(*refs, has_gate):
    it = iter(refs)
    x_ref, w_ref, wout_ref = next(it), next(it), next(it)
    vf_ref = next(it) if has_gate else None
    names = _PRM_NAMES + (_GATE_NAMES if has_gate else ())
    prm = {nme: next(it) for nme in names}
    lng_ref, lnb_ref = next(it), next(it)
    xo_ref, xob_ref = next(it), next(it)
    v_ref = None if has_gate else next(it)
    z_ref, tail_ref, s_ref = next(it), next(it), next(it)

    t = CHUNK
    c = pl.program_id(0)

    @pl.when(c == 0)
    def _():
        tail_ref[...] = jnp.zeros_like(tail_ref)
        s_ref[...] = jnp.zeros_like(s_ref)

    slabs = [(s, b) for s in range(TM_CHUNKS) for b in range(BATCH)]

    def gather_rows(ref):
        return jnp.concatenate([ref[b, s * t:(s + 1) * t, :] for s, b in slabs], axis=0)

    z_ref[...] = _dot(_bf(gather_rows(x_ref)), w_ref[...])

    def pair_tile(x, b, p):
        return x[b * t:(b + 1) * t, p * LANE:(p + 1) * LANE]

    order = [(b, p) for b in range(BATCH) for p in range(N_PAIRS)]
    lanes = [slice(p * LANE, (p + 1) * LANE) for _, p in order]
    gn_g = [prm["gn_g"][:, sl] for sl in lanes]
    gn_b = [prm["gn_b"][:, sl] for sl in lanes]
    r_k = [prm["r_k"][:, sl] for sl in lanes]

    states = [s_ref[i] for i in range(len(order))]
    y_rwkv, y_pool, values = [], [], []
    for s in range(TM_CHUNKS):
        z_s = z_ref.at[s * TM_ROWS:(s + 1) * TM_ROWS]
        if s == 0:
            def history(b, r0, r1, lo, width):
                return tail_ref[b, r0:r1, lo:lo + width]
        else:
            def history(b, r0, r1, lo, width, base=(s - 1) * TM_ROWS):
                top = base + (b + 1) * t - MAX_WINDOW
                return z_ref[top + r0:top + r1, lo:lo + width]
        vf = None
        if has_gate:
            vf = jnp.concatenate([vf_ref[b, s * t:(s + 1) * t, :] for b in range(BATCH)], axis=0)
        r, k, v, an_w, bb, cum, g, pool = _prep_rows(z_s, history, c * TM_CHUNKS + s, vf, prm,
                                                     has_gate)
        tiles = [tuple(pair_tile(x, b, p) for x in (r, k, v, an_w, bb, cum, g)) for b, p in order]
        outs, states = _recurrence_chunk(tiles, states, gn_g, gn_b, r_k)
        y_rwkv += [jnp.concatenate([_bf(outs[b * N_PAIRS + p]) for p in range(N_PAIRS)], axis=1)
                   for b in range(BATCH)]
        y_pool.append(_bf(pool))
        values.append(v)

    last = (TM_CHUNKS - 1) * TM_ROWS
    for b in range(BATCH):
        tail_ref[b] = z_ref[last + (b + 1) * t - MAX_WINDOW:last + (b + 1) * t, :]

    mix = (_dot(jnp.concatenate(y_rwkv, axis=0), wout_ref[0:RWKV_WIDTH, :])
           + _dot(jnp.concatenate(y_pool, axis=0), wout_ref[RWKV_WIDTH:, :]))
    x1 = _layer_norm_rows(ALPHA * gather_rows(x_ref) + mix, lng_ref[...], lnb_ref[...])

    for i in range(len(order)):
        s_ref[i] = states[i]
    for j, (s, b) in enumerate(slabs):
        xo_ref[b, s * t:(s + 1) * t, :] = x1[j * t:(j + 1) * t, :]
        xob_ref[b, s * t:(s + 1) * t, :] = _bf(x1[j * t:(j + 1) * t, :])
        if not has_gate:
            v_ref[b, s * t:(s + 1) * t, :] = values[s][b * t:(b + 1) * t, :]


def _time_mix(x, w_in_bf, w_out_bf, vf, p, ln_g, ln_b, layer):
    has_gate = layer > 0
    t = CHUNK * TM_CHUNKS

    def rows(width):
        return pl.BlockSpec((BATCH, t, width), lambda c: (0, c, 0))

    def full(shape):
        return pl.BlockSpec(shape, lambda c: (0,) * len(shape))

    def layer_of(shape):
        return pl.BlockSpec((None,) + shape, lambda c: (layer,) + (0,) * len(shape),
                            pipeline_mode=pl.Buffered(1))

    in_specs = [rows(D_MODEL), layer_of((D_MODEL, C_USED)), layer_of((D_MODEL, D_MODEL))]
    args = [x, w_in_bf, w_out_bf]
    if has_gate:
        in_specs.append(rows(RWKV_WIDTH))
        args.append(vf)
    for nme in _PRM_NAMES + (_GATE_NAMES if has_gate else ()):
        in_specs.append(full(p[nme].shape))
        args.append(p[nme])
    in_specs += [layer_of((1, D_MODEL))] * 2
    args += [ln_g, ln_b]
    out_shape = [jax.ShapeDtypeStruct((BATCH, SEQ, D_MODEL), F32),
                 jax.ShapeDtypeStruct((BATCH, SEQ, D_MODEL), BF16)]
    out_specs = [rows(D_MODEL), rows(D_MODEL)]
    if not has_gate:
        out_shape.append(jax.ShapeDtypeStruct((BATCH, SEQ, RWKV_WIDTH), F32))
        out_specs.append(rows(RWKV_WIDTH))
    return pl.pallas_call(
        functools.partial(_timemix_kernel, has_gate=has_gate),
        grid=(SEQ // t,),
        in_specs=in_specs,
        out_specs=out_specs,
        out_shape=out_shape,
        scratch_shapes=[pltpu.VMEM((TM_CHUNKS * TM_ROWS, C_USED), F32),
                        pltpu.VMEM((BATCH, MAX_WINDOW, C_USED), F32),
                        pltpu.VMEM((BATCH * N_PAIRS, 2 * HEAD_SIZE, LANE), F32)],
        compiler_params=pltpu.CompilerParams(
            dimension_semantics=("arbitrary",), vmem_limit_bytes=VMEM_LIMIT),
        name="time_mix_gate" if has_gate else "time_mix",
    )(*args)


MLP_ROWS = 1024
MLP_FF = 512
MLP_STEPS = D_FF // MLP_FF
MLP_RES_ROWS = MLP_ROWS // MLP_STEPS


MLP_WEIGHT_BUFFERS = 3


def _mlp_kernel(xb_ref, x_hbm, w1_hbm, w2_hbm, g_ref, b_ref, xo_ref, step_ref, *, layer):
    i = pl.program_id(0)
    xo_ref[...] = jnp.zeros_like(xo_ref)
    step_ref[0] = 0

    def step(xres_ref, w1_ref, w2_ref):
        f = step_ref[0]
        h = jnp.square(jnp.maximum(_dot(xb_ref[...], _bf(w1_ref[...])), 0.0))
        xo_ref[...] += _dot(_bf(h), _bf(w2_ref[...]))
        rows = pl.ds(pl.multiple_of(f * MLP_RES_ROWS, MLP_RES_ROWS), MLP_RES_ROWS)
        xo_ref[rows, :] += ALPHA * xres_ref[...]
        step_ref[0] = f + 1

    deep = pl.Buffered(MLP_WEIGHT_BUFFERS)
    pltpu.emit_pipeline(
        step,
        grid=(MLP_STEPS,),
        in_specs=[pl.BlockSpec((MLP_RES_ROWS, D_MODEL), lambda f: (i * MLP_STEPS + f, 0)),
                  pl.BlockSpec((None, D_MODEL, MLP_FF), lambda f: (layer, 0, f), pipeline_mode=deep),
                  pl.BlockSpec((None, MLP_FF, D_MODEL), lambda f: (layer, f, 0), pipeline_mode=deep)],
    )(x_hbm, w1_hbm, w2_hbm)

    xo_ref[...] = _layer_norm_rows(xo_ref[...], g_ref[...], b_ref[...])


def _mlp(x_bf, x, w1, w2, g, b, layer):
    m = x.shape[0]
    row_spec = pl.BlockSpec((MLP_ROWS, D_MODEL), lambda i: (i, 0))
    vec_spec = pl.BlockSpec((None, 1, D_MODEL), lambda i: (layer, 0, 0))
    hbm_spec = pl.BlockSpec(memory_space=pl.ANY)
    return pl.pallas_call(
        functools.partial(_mlp_kernel, layer=layer),
        grid=(m // MLP_ROWS,),
        in_specs=[row_spec, hbm_spec, hbm_spec, hbm_spec, vec_spec, vec_spec],
        out_specs=row_spec,
        out_shape=jax.ShapeDtypeStruct((m, D_MODEL), F32),
        scratch_shapes=[pltpu.SMEM((1,), jnp.int32)],
        compiler_params=pltpu.CompilerParams(
            dimension_semantics=("arbitrary",), vmem_limit_bytes=VMEM_LIMIT),
        name="mlp_ln",
    )(x_bf, x, w1, w2, g, b)


W_PREP_ROWS = 256


def _w_in_prep_kernel(wt_ref, o_ref):
    gap = jnp.zeros((C_POOL - C_SHIFT, W_PREP_ROWS), F32)
    padded = jnp.concatenate([wt_ref[0:C_SHIFT, :], gap, wt_ref[C_SHIFT:C_IN, :]], axis=0)
    o_ref[...] = _bf(padded.T)


def _w_in_prep(w_in):
    w_t = jnp.swapaxes(w_in, 1, 2)
    return pl.pallas_call(
        _w_in_prep_kernel,
        grid=(DEPTH, D_MODEL // W_PREP_ROWS),
        in_specs=[pl.BlockSpec((None, C_IN, W_PREP_ROWS), lambda l, i: (l, 0, i))],
        out_specs=pl.BlockSpec((None, W_PREP_ROWS, C_USED), lambda l, i: (l, i, 0)),
        out_shape=jax.ShapeDtypeStruct((DEPTH, D_MODEL, C_USED), BF16),
        compiler_params=pltpu.CompilerParams(
            dimension_semantics=("arbitrary", "arbitrary"), vmem_limit_bytes=VMEM_LIMIT),
        name="w_in_prep",
    )(w_t)


def _pad_rows(w, rows, at=0):
    out = jnp.zeros((rows, w.shape[1]), w.dtype)
    return out.at[at:at + w.shape[0]].set(w)


def _layer_params(l, mu, w0, w_up, a0, a_up, g_up, v0, v_down, v_up, k_k, k_a, r_k,
                  gn_g, gn_b, pool_w, pool_scale):
    p = {
        "mu": jnp.concatenate([mu[l], jnp.zeros((C_USED - C_SHIFT,), F32)])[None, :],
        "w0": w0[l][None, :],
        "w_up": _bf(_pad_rows(w_up[l], LANE, 0)),
        "a0": a0[l][None, :],
        "a_up": _bf(_pad_rows(a_up[l], LANE, R_DECAY)),
        "g_up": _bf(_pad_rows(g_up[l], 2 * LANE, 0)),
        "k_k": k_k[l][None, :],
        "k_a": k_a[l][None, :],
        "pool_w": _bf(pool_w[l]),
        "pool_scale": pool_scale[l][None, :],
        "r_k": r_k[l].reshape(1, RWKV_WIDTH),
        "gn_g": gn_g[l][None, :],
        "gn_b": gn_b[l][None, :],
    }
    if l > 0:
        p["v0"] = v0[l - 1][None, :]
        p["v_down"] = _bf(jnp.pad(v_down[l - 1], ((0, 0), (0, LANE - R_MV))))
        p["v_up"] = _bf(_pad_rows(v_up[l - 1], LANE, 0))
    return p


def kernel(x, w_in, mu, w0, w_up, a0, a_up, g_up, v0, v_down, v_up, k_k, k_a, r_k, gn_g, gn_b,
           pool_w, pool_scale, w_out, ln1_g, ln1_b, mlp_w1, mlp_w2, ln2_g, ln2_b):
    m = BATCH * SEQ
    w_in_bf = _w_in_prep(w_in)
    w_out_bf = _bf(w_out)
    ln1_g, ln1_b, ln2_g, ln2_b = (v.reshape(DEPTH, 1, D_MODEL) for v in (ln1_g, ln1_b, ln2_g, ln2_b))
    v_first = None
    for l in range(DEPTH):
        p = _layer_params(l, mu, w0, w_up, a0, a_up, g_up, v0, v_down, v_up, k_k, k_a,
                          r_k, gn_g, gn_b, pool_w, pool_scale)
        res = _time_mix(x, w_in_bf, w_out_bf, v_first, p, ln1_g, ln1_b, l)
        x, x_bf = res[0], res[1]
        if l == 0:
            v_first = res[2]
        x = _mlp(x_bf.reshape(m, D_MODEL), x.reshape(m, D_MODEL), mlp_w1, mlp_w2,
                 ln2_g, ln2_b, l).reshape(BATCH, SEQ, D_MODEL)
    return x
```

```python
import functools

import jax
import jax.numpy as jnp
from jax import lax
from jax.experimental import pallas as pl
from jax.experimental.pallas import tpu as pltpu

D_MODEL = 2048
BATCH = 2
SEQ = 8192
DEPTH = 2
RWKV_WIDTH = 1024
POOL_WIDTH = 1024
HEAD_SIZE = 64
POOL_WINDOWS = (2, 4, 8, 16)
POOL_GROUP = 256
R_DECAY = 64
R_AAA = 64
R_MV = 32
R_GATE = 160
C_SHIFT = 3 * RWKV_WIDTH + R_DECAY + R_AAA + R_GATE
C_IN = C_SHIFT + POOL_WIDTH
D_FF = 4 * D_MODEL
ALPHA = (2.0 * DEPTH) ** 0.25
LN_EPS = 1e-5
GN_EPS = 64e-5
L2_EPS = 1e-12

LANE = 128
CHUNK = 64
N_PAIRS = RWKV_WIDTH // LANE
MAX_WINDOW = max(POOL_WINDOWS)

C_LR = 3 * RWKV_WIDTH
C_LR_W = 3 * LANE
C_POOL = C_LR + C_LR_W
C_USED = C_POOL + POOL_WIDTH

VMEM_LIMIT = 58 * 1024 * 1024

F32 = jnp.float32
BF16 = jnp.bfloat16
F32_PIECES = 3


def _bf(x):
    return x.astype(BF16)


def _dot(a, b):
    return jnp.dot(a, b, preferred_element_type=F32)


def _dot_nt(a, b):
    return lax.dot_general(a, b, (((1,), (1,)), ((), ())), preferred_element_type=F32)


def _dot_tn(a, b):
    return lax.dot_general(a, b, (((0,), (0,)), ((), ())), preferred_element_type=F32)


def _split_dot(x, w_bf, n_terms):
    acc = None
    rem = x
    for _ in range(n_terms):
        piece = _bf(rem)
        term = _dot(piece, w_bf)
        acc = term if acc is None else acc + term
        rem = rem - piece.astype(F32)
    return acc


def _split_dot_left(w_bf, x, n_terms):
    acc = None
    rem = x
    for _ in range(n_terms):
        piece = _bf(rem)
        term = _dot(w_bf, piece)
        acc = term if acc is None else acc + term
        rem = rem - piece.astype(F32)
    return acc


def _sigmoid(x):
    return 1.0 / (1.0 + jnp.exp(-x))


def _layer_norm_rows(h, g, b):
    mean = jnp.mean(h, axis=-1, keepdims=True)
    d = h - mean
    var = jnp.mean(d * d, axis=-1, keepdims=True)
    return d * lax.rsqrt(var + LN_EPS) * g + b


TM_ROWS = BATCH * CHUNK
TM_CHUNKS = 2


def _prep_rows(z_ref, history, chunk, vf, prm, has_gate):
    t = CHUNK

    def token_shift_mix(lo, width):
        cur = z_ref[:, lo:lo + width]
        row = lax.broadcasted_iota(jnp.int32, cur.shape, 0)
        prev = pltpu.roll(cur, 1, axis=0)
        for b in range(BATCH):
            last = history(b, MAX_WINDOW - 1, MAX_WINDOW, lo, width)
            prev = jnp.where(row == b * t, last, prev)
        return cur + (prev - cur) * prm["mu"][:, lo:lo + width]

    zl = token_shift_mix(C_LR, C_LR_W)
    xwa = zl[:, 0:LANE]
    xg = zl[:, LANE:C_LR_W]
    w_pre = prm["w0"][...] + _dot(_bf(jnp.tanh(xwa)), prm["w_up"][...])
    lw = -jnp.exp(-0.5) * _sigmoid(w_pre)
    a = _sigmoid(prm["a0"][...] + _dot(_bf(xwa), prm["a_up"][...]))
    g = _dot(_bf(_sigmoid(xg)), prm["g_up"][...])

    rr = lax.broadcasted_iota(jnp.int32, (TM_ROWS, TM_ROWS), 0)
    cc = lax.broadcasted_iota(jnp.int32, (TM_ROWS, TM_ROWS), 1)
    chunk_bits = t.bit_length() - 1
    same_chunk = jnp.right_shift(rr, chunk_bits) == jnp.right_shift(cc, chunk_bits)
    tri = jnp.where((rr >= cc) & same_chunk, 1.0, 0.0).astype(BF16)
    cum = _split_dot_left(tri, lw, F32_PIECES)

    r = token_shift_mix(0, RWKV_WIDTH)
    k = token_shift_mix(RWKV_WIDTH, RWKV_WIDTH)
    v = token_shift_mix(2 * RWKV_WIDTH, RWKV_WIDTH)
    if has_gate:
        low = _dot(_bf(v), prm["v_down"][...])
        gate = _sigmoid(prm["v0"][...] + _dot(_bf(low), prm["v_up"][...]))
        v = v + (vf - v) * gate

    kk = k * prm["k_k"][...]
    sr = lax.broadcasted_iota(jnp.int32, (2 * LANE, 2 * LANE), 0)
    sc = lax.broadcasted_iota(jnp.int32, (2 * LANE, 2 * LANE), 1)
    head_bits = HEAD_SIZE.bit_length() - 1
    seg = jnp.where(jnp.right_shift(sr, head_bits) == jnp.right_shift(sc, head_bits),
                    1.0, 0.0).astype(BF16)
    sq = kk * kk
    ss = jnp.concatenate(
        [_split_dot(sq[:, j:j + 2 * LANE], seg, F32_PIECES)
         for j in range(0, RWKV_WIDTH, 2 * LANE)], axis=1)
    kkn = kk / jnp.maximum(jnp.sqrt(ss), L2_EPS)
    an_w = -kkn * jnp.exp(-lw)
    bb = kkn * a
    k_mod = k * (1.0 + (a - 1.0) * prm["k_a"][...])

    pos = chunk * t + lax.broadcasted_iota(jnp.int32, (t, POOL_GROUP), 0) + 1
    outs = []
    for gi, win in enumerate(POOL_WINDOWS):
        lo = C_POOL + gi * POOL_GROUP
        cnt = jnp.minimum(pos, win).astype(F32)
        d_rows = []
        for b in range(BATCH):
            u = z_ref[b * t:(b + 1) * t, lo:lo + POOL_GROUP]
            e = jnp.concatenate([history(b, 0, MAX_WINDOW, lo, POOL_GROUP), u], axis=0)
            step = 1
            while step < win:
                e = e + pltpu.roll(e, step, axis=0)
                step *= 2
            d_rows.append(e[MAX_WINDOW:MAX_WINDOW + t, :] / cnt - u)
        outs.append(_dot(_bf(jnp.concatenate(d_rows, axis=0)), prm["pool_w"][gi]))
    y_pool = jnp.concatenate(outs, axis=1) * prm["pool_scale"][...]
    return r, k_mod, v, an_w, bb, cum, g, y_pool


def _recurrence_chunk(tiles, states, gn_g, gn_b, r_k):
    t = CHUNK
    lane = lax.broadcasted_iota(jnp.int32, (t, LANE), 1)
    row = lax.broadcasted_iota(jnp.int32, (t, LANE), 0)
    head0 = lane < HEAD_SIZE
    col = jnp.bitwise_and(lane, t - 1)
    strict = row > col
    incl = row >= col
    eye = jnp.where(row == col, 1.0, 0.0)

    def stack(x):
        return jnp.concatenate([jnp.where(head0, x, 0.0), jnp.where(head0, 0.0, x)], axis=0)

    def head_sums(x):
        s0 = jnp.sum(jnp.where(head0, x, 0.0), axis=-1, keepdims=True)
        s1 = jnp.sum(jnp.where(head0, 0.0, x), axis=-1, keepdims=True)
        return jnp.where(head0, s0, s1)

    rs, ks, vs, ans, bbs, cums, gs = (list(x) for x in zip(*tiles))
    pairs = range(len(tiles))

    cum_ends = [c[t - 1:t, :] for c in cums]
    decay_in = [jnp.exp(c) for c in cums]
    decay_out = [jnp.exp(-c) for c in cums]
    decay_rest = [jnp.exp(e - c) for c, e in zip(cums, cum_ends)]
    v_st = [_bf(stack(v)) for v in vs]
    ar = [jnp.concatenate([_bf(a * d), _bf(r * d)], axis=0)
          for a, r, d in zip(ans, rs, decay_in)]
    bk_st = [jnp.concatenate([_bf(stack(b * d)), _bf(stack(k * d))], axis=0)
             for b, k, d in zip(bbs, ks, decay_out)]
    bk_end = [jnp.concatenate([_bf(stack(b * d)), _bf(stack(k * d))], axis=0)
              for b, k, d in zip(bbs, ks, decay_rest)]

    scores = [_dot_nt(x, y) for x, y in zip(ar, bk_st)]
    l_ab = [jnp.where(strict, sc[:t, :LANE], 0.0) for sc in scores]
    l_ak = [jnp.where(strict, sc[:t, LANE:], 0.0) for sc in scores]
    m_rbk = [jnp.concatenate([_bf(jnp.where(incl, sc[t:, :LANE], 0.0)),
                              _bf(jnp.where(incl, sc[t:, LANE:], 0.0))], axis=1) for sc in scores]
    from_state = [_dot_nt(x, _bf(st)) for x, st in zip(ar, states)]
    rhs = [fs[:t] + _dot(_bf(l), vst) for fs, l, vst in zip(from_state, l_ak, v_st)]

    power = l_ab
    inv = [eye + l for l in l_ab]
    for _ in range(5):
        power = [_dot(_bf(x), _bf(stack(x))) for x in power]
        inv = [x + _dot(_bf(x), _bf(stack(pw))) for x, pw in zip(inv, power)]
    u = [_dot(_bf(x), _bf(stack(y))) for x, y in zip(inv, rhs)]
    uv_st = [jnp.concatenate([_bf(stack(x)), vst], axis=0) for x, vst in zip(u, v_st)]
    ys = [fs[t:] + _dot(m, x) for fs, m, x in zip(from_state, m_rbk, uv_st)]
    new_states = [st * jnp.exp(e) + _dot_tn(x, be)
                  for st, e, x, be in zip(states, cum_ends, uv_st, bk_end)]

    outs = []
    for p in pairs:
        d = ys[p] - head_sums(ys[p]) * (1.0 / HEAD_SIZE)
        var = head_sums(d * d) * (1.0 / HEAD_SIZE)
        yn = d * lax.rsqrt(var + GN_EPS)
        bonus = head_sums(rs[p] * ks[p] * r_k[p]) * vs[p]
        outs.append((yn * gn_g[p] + gn_b[p] + bonus) * gs[p])
    return outs, new_states


_PRM_NAMES = ("mu", "w0", "w_up", "a0", "a_up", "g_up", "k_k", "k_a", "pool_w", "pool_scale",
              "gn_g", "gn_b", "r_k")
_GATE_NAMES = ("v0", "v_down", "v_up")


def _timemix_kernel(*refs, has_gate):
    it = iter(refs)
    x_ref, w_ref, wout_ref = next(it), next(it), next(it)
    vf_ref = next(it) if has_gate else None
    names = _PRM_NAMES + (_GATE_NAMES if has_gate else ())
    prm = {nme: next(it) for nme in names}
    lng_ref, lnb_ref = next(it), next(it)
    xo_ref, xob_ref = next(it), next(it)
    v_ref = None if has_gate else next(it)
    z_ref, tail_ref, s_ref = next(it), next(it), next(it)

    t = CHUNK
    c = pl.program_id(0)

    @pl.when(c == 0)
    def _():
        tail_ref[...] = jnp.zeros_like(tail_ref)
        s_ref[...] = jnp.zeros_like(s_ref)

    slabs = [(s, b) for s in range(TM_CHUNKS) for b in range(BATCH)]

    def gather_rows(ref):
        return jnp.concatenate([ref[b, s * t:(s + 1) * t, :] for s, b in slabs], axis=0)

    z_ref[...] = _dot(_bf(gather_rows(x_ref)), w_ref[...])

    def pair_tile(x, b, p):
        return x[b * t:(b + 1) * t, p * LANE:(p + 1) * LANE]

    order = [(b, p) for b in range(BATCH) for p in range(N_PAIRS)]
    lanes = [slice(p * LANE, (p + 1) * LANE) for _, p in order]
    gn_g = [prm["gn_g"][:, sl] for sl in lanes]
    gn_b = [prm["gn_b"][:, sl] for sl in lanes]
    r_k = [prm["r_k"][:, sl] for sl in lanes]

    states = [s_ref[i] for i in range(len(order))]
    y_rwkv, y_pool, values = [], [], []
    for s in range(TM_CHUNKS):
        z_s = z_ref.at[s * TM_ROWS:(s + 1) * TM_ROWS]
        if s == 0:
            def history(b, r0, r1, lo, width):
                return tail_ref[b, r0:r1, lo:lo + width]
        else:
            def history(b, r0, r1, lo, width, base=(s - 1) * TM_ROWS):
                top = base + (b + 1) * t - MAX_WINDOW
                return z_ref[top + r0:top + r1, lo:lo + width]
        vf = None
        if has_gate:
            vf = jnp.concatenate([vf_ref[b, s * t:(s + 1) * t, :] for b in range(BATCH)], axis=0)
        r, k, v, an_w, bb, cum, g, pool = _prep_rows(z_s, history, c * TM_CHUNKS + s, vf, prm,
                                                     has_gate)
        tiles = [tuple(pair_tile(x, b, p) for x in (r, k, v, an_w, bb, cum, g)) for b, p in order]
        outs, states = _recurrence_chunk(tiles, states, gn_g, gn_b, r_k)
        y_rwkv += [jnp.concatenate([_bf(outs[b * N_PAIRS + p]) for p in range(N_PAIRS)], axis=1)
                   for b in range(BATCH)]
        y_pool.append(_bf(pool))
        values.append(v)

    last = (TM_CHUNKS - 1) * TM_ROWS
    for b in range(BATCH):
        tail_ref[b] = z_ref[last + (b + 1) * t - MAX_WINDOW:last + (b + 1) * t, :]

    mix = (_dot(jnp.concatenate(y_rwkv, axis=0), wout_ref[0:RWKV_WIDTH, :])
           + _dot(jnp.concatenate(y_pool, axis=0), wout_ref[RWKV_WIDTH:, :]))
    x1 = _layer_norm_rows(ALPHA * gather_rows(x_ref) + mix, lng_ref[...], lnb_ref[...])

    for i in range(len(order)):
        s_ref[i] = states[i]
    for j, (s, b) in enumerate(slabs):
        xo_ref[b, s * t:(s + 1) * t, :] = x1[j * t:(j + 1) * t, :]
        xob_ref[b, s * t:(s + 1) * t, :] = _bf(x1[j * t:(j + 1) * t, :])
        if not has_gate:
            v_ref[b, s * t:(s + 1) * t, :] = values[s][b * t:(b + 1) * t, :]


def _time_mix(x, w_in_bf, w_out_bf, vf, p, ln_g, ln_b, layer):
    has_gate = layer > 0
    t = CHUNK * TM_CHUNKS

    def rows(width):
        return pl.BlockSpec((BATCH, t, width), lambda c: (0, c, 0))

    def full(shape):
        return pl.BlockSpec(shape, lambda c: (0,) * len(shape))

    def layer_of(shape):
        return pl.BlockSpec((None,) + shape, lambda c: (layer,) + (0,) * len(shape),
                            pipeline_mode=pl.Buffered(1))

    in_specs = [rows(D_MODEL), layer_of((D_MODEL, C_USED)), layer_of((D_MODEL, D_MODEL))]
    args = [x, w_in_bf, w_out_bf]
    if has_gate:
        in_specs.append(rows(RWKV_WIDTH))
        args.append(vf)
    for nme in _PRM_NAMES + (_GATE_NAMES if has_gate else ()):
        in_specs.append(full(p[nme].shape))
        args.append(p[nme])
    in_specs += [layer_of((1, D_MODEL))] * 2
    args += [ln_g, ln_b]
    out_shape = [jax.ShapeDtypeStruct((BATCH, SEQ, D_MODEL), F32),
                 jax.ShapeDtypeStruct((BATCH, SEQ, D_MODEL), BF16)]
    out_specs = [rows(D_MODEL), rows(D_MODEL)]
    if not has_gate:
        out_shape.append(jax.ShapeDtypeStruct((BATCH, SEQ, RWKV_WIDTH), F32))
        out_specs.append(rows(RWKV_WIDTH))
    return pl.pallas_call(
        functools.partial(_timemix_kernel, has_gate=has_gate),
        grid=(SEQ // t,),
        in_specs=in_specs,
        out_specs=out_specs,
        out_shape=out_shape,
        scratch_shapes=[pltpu.VMEM((TM_CHUNKS * TM_ROWS, C_USED), F32),
                        pltpu.VMEM((BATCH, MAX_WINDOW, C_USED), F32),
                        pltpu.VMEM((BATCH * N_PAIRS, 2 * HEAD_SIZE, LANE), F32)],
        compiler_params=pltpu.CompilerParams(
            dimension_semantics=("arbitrary",), vmem_limit_bytes=VMEM_LIMIT),
        name="time_mix_gate" if has_gate else "time_mix",
    )(*args)


MLP_ROWS = 1024
MLP_FF = 512
MLP_STEPS = D_FF // MLP_FF
MLP_RES_ROWS = MLP_ROWS // MLP_STEPS


def _mlp_kernel(xb_ref, xres_ref, w1_ref, w2_ref, g_ref, b_ref, xo_ref):
    f = pl.program_id(1)

    @pl.when(f == 0)
    def _():
        xo_ref[...] = jnp.zeros_like(xo_ref)

    h = jnp.square(jnp.maximum(_dot(xb_ref[...], _bf(w1_ref[...])), 0.0))
    xo_ref[...] += _dot(_bf(h), _bf(w2_ref[...]))
    rows = pl.ds(pl.multiple_of(f * MLP_RES_ROWS, MLP_RES_ROWS), MLP_RES_ROWS)
    xo_ref[rows, :] += ALPHA * xres_ref[...]

    @pl.when(f == MLP_STEPS - 1)
    def _():
        xo_ref[...] = _layer_norm_rows(xo_ref[...], g_ref[...], b_ref[...])


def _mlp(x_bf, x, w1, w2, g, b, layer):
    m = x.shape[0]
    bm, bf = MLP_ROWS, MLP_FF
    row_spec = pl.BlockSpec((bm, D_MODEL), lambda i, f: (i, 0))
    vec_spec = pl.BlockSpec((None, 1, D_MODEL), lambda i, f: (layer, 0, 0))

    def hidden_block(i, f):
        return jnp.where(i % 2 == 1, MLP_STEPS - 1 - f, f)

    return pl.pallas_call(
        _mlp_kernel,
        grid=(m // bm, MLP_STEPS),
        in_specs=[row_spec,
                  pl.BlockSpec((MLP_RES_ROWS, D_MODEL), lambda i, f: (i * MLP_STEPS + f, 0)),
                  pl.BlockSpec((None, D_MODEL, bf), lambda i, f: (layer, 0, hidden_block(i, f))),
                  pl.BlockSpec((None, bf, D_MODEL), lambda i, f: (layer, hidden_block(i, f), 0)),
                  vec_spec, vec_spec],
        out_specs=row_spec,
        out_shape=jax.ShapeDtypeStruct((m, D_MODEL), F32),
        compiler_params=pltpu.CompilerParams(
            dimension_semantics=("arbitrary", "arbitrary"), vmem_limit_bytes=VMEM_LIMIT),
        name="mlp_ln",
    )(x_bf, x, w1, w2, g, b)


W_PREP_ROWS = 256


def _w_in_prep_kernel(wt_ref, o_ref):
    gap = jnp.zeros((C_POOL - C_SHIFT, W_PREP_ROWS), F32)
    padded = jnp.concatenate([wt_ref[0:C_SHIFT, :], gap, wt_ref[C_SHIFT:C_IN, :]], axis=0)
    o_ref[...] = _bf(padded.T)


def _w_in_prep(w_in):
    w_t = jnp.swapaxes(w_in, 1, 2)
    return pl.pallas_call(
        _w_in_prep_kernel,
        grid=(DEPTH, D_MODEL // W_PREP_ROWS),
        in_specs=[pl.BlockSpec((None, C_IN, W_PREP_ROWS), lambda l, i: (l, 0, i))],
        out_specs=pl.BlockSpec((None, W_PREP_ROWS, C_USED), lambda l, i: (l, i, 0)),
        out_shape=jax.ShapeDtypeStruct((DEPTH, D_MODEL, C_USED), BF16),
        compiler_params=pltpu.CompilerParams(
            dimension_semantics=("arbitrary", "arbitrary"), vmem_limit_bytes=VMEM_LIMIT),
        name="w_in_prep",
    )(w_t)


def _pad_rows(w, rows, at=0):
    out = jnp.zeros((rows, w.shape[1]), w.dtype)
    return out.at[at:at + w.shape[0]].set(w)


def _layer_params(l, mu, w0, w_up, a0, a_up, g_up, v0, v_down, v_up, k_k, k_a, r_k,
                  gn_g, gn_b, pool_w, pool_scale):
    p = {
        "mu": jnp.concatenate([mu[l], jnp.zeros((C_USED - C_SHIFT,), F32)])[None, :],
        "w0": w0[l][None, :],
        "w_up": _bf(_pad_rows(w_up[l], LANE, 0)),
        "a0": a0[l][None, :],
        "a_up": _bf(_pad_rows(a_up[l], LANE, R_DECAY)),
        "g_up": _bf(_pad_rows(g_up[l], 2 * LANE, 0)),
        "k_k": k_k[l][None, :],
        "k_a": k_a[l][None, :],
        "pool_w": _bf(pool_w[l]),
        "pool_scale": pool_scale[l][None, :],
        "r_k": r_k[l].reshape(1, RWKV_WIDTH),
        "gn_g": gn_g[l][None, :],
        "gn_b": gn_b[l][None, :],
    }
    if l > 0:
        p["v0"] = v0[l - 1][None, :]
        p["v_down"] = _bf(jnp.pad(v_down[l - 1], ((0, 0), (0, LANE - R_MV))))
        p["v_up"] = _bf(_pad_rows(v_up[l - 1], LANE, 0))
    return p


def kernel(x, w_in, mu, w0, w_up, a0, a_up, g_up, v0, v_down, v_up, k_k, k_a, r_k, gn_g, gn_b,
           pool_w, pool_scale, w_out, ln1_g, ln1_b, mlp_w1, mlp_w2, ln2_g, ln2_b):
    m = BATCH * SEQ
    w_in_bf = _w_in_prep(w_in)
    w_out_bf = _bf(w_out)
    ln1_g, ln1_b, ln2_g, ln2_b = (v.reshape(DEPTH, 1, D_MODEL) for v in (ln1_g, ln1_b, ln2_g, ln2_b))
    v_first = None
    for l in range(DEPTH):
        p = _layer_params(l, mu, w0, w_up, a0, a_up, g_up, v0, v_down, v_up, k_k, k_a,
                          r_k, gn_g, gn_b, pool_w, pool_scale)
        res = _time_mix(x, w_in_bf, w_out_bf, v_first, p, ln1_g, ln1_b, l)
        x, x_bf = res[0], res[1]
        if l == 0:
            v_first = res[2]
        x = _mlp(x_bf.reshape(m, D_MODEL), x.reshape(m, D_MODEL), mlp_w1, mlp_w2,
                 ln2_g, ln2_b, l).reshape(BATCH, SEQ, D_MODEL)
    return x
```

```python
import functools

import jax
import jax.numpy as jnp
from jax import lax
from jax.experimental import pallas as pl
from jax.experimental.pallas import tpu as pltpu

D_MODEL = 2048
BATCH = 2
SEQ = 8192
DEPTH = 2
RWKV_WIDTH = 1024
POOL_WIDTH = 1024
HEAD_SIZE = 64
POOL_WINDOWS = (2, 4, 8, 16)
POOL_GROUP = 256
R_DECAY = 64
R_AAA = 64
R_MV = 32
R_GATE = 160
C_SHIFT = 3 * RWKV_WIDTH + R_DECAY + R_AAA + R_GATE
C_IN = C_SHIFT + POOL_WIDTH
D_FF = 4 * D_MODEL
ALPHA = (2.0 * DEPTH) ** 0.25
LN_EPS = 1e-5
GN_EPS = 64e-5
L2_EPS = 1e-12

LANE = 128
CHUNK = 64
N_PAIRS = RWKV_WIDTH // LANE
MAX_WINDOW = max(POOL_WINDOWS)

C_LR = 3 * RWKV_WIDTH
C_LR_W = 3 * LANE
C_POOL = C_LR + C_LR_W
C_USED = C_POOL + POOL_WIDTH

VMEM_LIMIT = 58 * 1024 * 1024

F32 = jnp.float32
BF16 = jnp.bfloat16
F32_PIECES = 3


def _bf(x):
    return x.astype(BF16)


def _dot(a, b):
    return jnp.dot(a, b, preferred_element_type=F32)


def _dot_nt(a, b):
    return lax.dot_general(a, b, (((1,), (1,)), ((), ())), preferred_element_type=F32)


def _dot_tn(a, b):
    return lax.dot_general(a, b, (((0,), (0,)), ((), ())), preferred_element_type=F32)


def _split_dot(x, w_bf, n_terms):
    acc = None
    rem = x
    for _ in range(n_terms):
        piece = _bf(rem)
        term = _dot(piece, w_bf)
        acc = term if acc is None else acc + term
        rem = rem - piece.astype(F32)
    return acc


def _split_dot_left(w_bf, x, n_terms):
    acc = None
    rem = x
    for _ in range(n_terms):
        piece = _bf(rem)
        term = _dot(w_bf, piece)
        acc = term if acc is None else acc + term
        rem = rem - piece.astype(F32)
    return acc


def _sigmoid(x):
    return 1.0 / (1.0 + jnp.exp(-x))


def _layer_norm_rows(h, g, b):
    mean = jnp.mean(h, axis=-1, keepdims=True)
    d = h - mean
    var = jnp.mean(d * d, axis=-1, keepdims=True)
    return d * lax.rsqrt(var + LN_EPS) * g + b


TM_ROWS = BATCH * CHUNK
TM_CHUNKS = 2


def _prep_rows(z_ref, history, chunk, vf, prm, has_gate):
    t = CHUNK

    def token_shift_mix(lo, width):
        cur = z_ref[:, lo:lo + width]
        row = lax.broadcasted_iota(jnp.int32, cur.shape, 0)
        prev = pltpu.roll(cur, 1, axis=0)
        for b in range(BATCH):
            last = history(b, MAX_WINDOW - 1, MAX_WINDOW, lo, width)
            prev = jnp.where(row == b * t, last, prev)
        return cur + (prev - cur) * prm["mu"][:, lo:lo + width]

    zl = token_shift_mix(C_LR, C_LR_W)
    xwa = zl[:, 0:LANE]
    xg = zl[:, LANE:C_LR_W]
    w_pre = prm["w0"][...] + _dot(_bf(jnp.tanh(xwa)), prm["w_up"][...])
    lw = -jnp.exp(-0.5) * _sigmoid(w_pre)
    a = _sigmoid(prm["a0"][...] + _dot(_bf(xwa), prm["a_up"][...]))
    g = _dot(_bf(_sigmoid(xg)), prm["g_up"][...])

    rr = lax.broadcasted_iota(jnp.int32, (TM_ROWS, TM_ROWS), 0)
    cc = lax.broadcasted_iota(jnp.int32, (TM_ROWS, TM_ROWS), 1)
    chunk_bits = t.bit_length() - 1
    same_chunk = jnp.right_shift(rr, chunk_bits) == jnp.right_shift(cc, chunk_bits)
    tri = jnp.where((rr >= cc) & same_chunk, 1.0, 0.0).astype(BF16)
    cum = _split_dot_left(tri, lw, F32_PIECES)

    r = token_shift_mix(0, RWKV_WIDTH)
    k = token_shift_mix(RWKV_WIDTH, RWKV_WIDTH)
    v = token_shift_mix(2 * RWKV_WIDTH, RWKV_WIDTH)
    if has_gate:
        low = _dot(_bf(v), prm["v_down"][...])
        gate = _sigmoid(prm["v0"][...] + _dot(_bf(low), prm["v_up"][...]))
        v = v + (vf - v) * gate

    kk = k * prm["k_k"][...]
    sr = lax.broadcasted_iota(jnp.int32, (2 * LANE, 2 * LANE), 0)
    sc = lax.broadcasted_iota(jnp.int32, (2 * LANE, 2 * LANE), 1)
    head_bits = HEAD_SIZE.bit_length() - 1
    seg = jnp.where(jnp.right_shift(sr, head_bits) == jnp.right_shift(sc, head_bits),
                    1.0, 0.0).astype(BF16)
    sq = kk * kk
    ss = jnp.concatenate(
        [_split_dot(sq[:, j:j + 2 * LANE], seg, F32_PIECES)
         for j in range(0, RWKV_WIDTH, 2 * LANE)], axis=1)
    kkn = kk / jnp.maximum(jnp.sqrt(ss), L2_EPS)
    an_w = -kkn * jnp.exp(-lw)
    bb = kkn * a
    k_mod = k * (1.0 + (a - 1.0) * prm["k_a"][...])

    pos = chunk * t + lax.broadcasted_iota(jnp.int32, (t, POOL_GROUP), 0) + 1
    outs = []
    for gi, win in enumerate(POOL_WINDOWS):
        lo = C_POOL + gi * POOL_GROUP
        cnt = jnp.minimum(pos, win).astype(F32)
        d_rows = []
        for b in range(BATCH):
            u = z_ref[b * t:(b + 1) * t, lo:lo + POOL_GROUP]
            e = jnp.concatenate([history(b, 0, MAX_WINDOW, lo, POOL_GROUP), u], axis=0)
            step = 1
            while step < win:
                e = e + pltpu.roll(e, step, axis=0)
                step *= 2
            d_rows.append(e[MAX_WINDOW:MAX_WINDOW + t, :] / cnt - u)
        outs.append(_dot(_bf(jnp.concatenate(d_rows, axis=0)), prm["pool_w"][gi]))
    y_pool = jnp.concatenate(outs, axis=1) * prm["pool_scale"][...]
    return r, k_mod, v, an_w, bb, cum, g, y_pool


def _recurrence_chunk(tiles, states, gn_g, gn_b, r_k):
    t = CHUNK
    lane = lax.broadcasted_iota(jnp.int32, (t, LANE), 1)
    row = lax.broadcasted_iota(jnp.int32, (t, LANE), 0)
    head0 = lane < HEAD_SIZE
    col = jnp.bitwise_and(lane, t - 1)
    strict = row > col
    incl = row >= col
    eye = jnp.where(row == col, 1.0, 0.0)

    def stack(x):
        return jnp.concatenate([jnp.where(head0, x, 0.0), jnp.where(head0, 0.0, x)], axis=0)

    def head_sums(x):
        s0 = jnp.sum(jnp.where(head0, x, 0.0), axis=-1, keepdims=True)
        s1 = jnp.sum(jnp.where(head0, 0.0, x), axis=-1, keepdims=True)
        return jnp.where(head0, s0, s1)

    rs, ks, vs, ans, bbs, cums, gs = (list(x) for x in zip(*tiles))
    pairs = range(len(tiles))

    cum_ends = [c[t - 1:t, :] for c in cums]
    decay_in = [jnp.exp(c) for c in cums]
    decay_out = [jnp.exp(-c) for c in cums]
    decay_rest = [jnp.exp(e - c) for c, e in zip(cums, cum_ends)]
    v_st = [_bf(stack(v)) for v in vs]
    ar = [jnp.concatenate([_bf(a * d), _bf(r * d)], axis=0)
          for a, r, d in zip(ans, rs, decay_in)]
    bk_st = [jnp.concatenate([_bf(stack(b * d)), _bf(stack(k * d))], axis=0)
             for b, k, d in zip(bbs, ks, decay_out)]
    bk_end = [jnp.concatenate([_bf(stack(b * d)), _bf(stack(k * d))], axis=0)
              for b, k, d in zip(bbs, ks, decay_rest)]

    scores = [_dot_nt(x, y) for x, y in zip(ar, bk_st)]
    l_ab = [jnp.where(strict, sc[:t, :LANE], 0.0) for sc in scores]
    l_ak = [jnp.where(strict, sc[:t, LANE:], 0.0) for sc in scores]
    m_rbk = [jnp.concatenate([_bf(jnp.where(incl, sc[t:, :LANE], 0.0)),
                              _bf(jnp.where(incl, sc[t:, LANE:], 0.0))], axis=1) for sc in scores]
    from_state = [_dot_nt(x, _bf(st)) for x, st in zip(ar, states)]
    rhs = [fs[:t] + _dot(_bf(l), vst) for fs, l, vst in zip(from_state, l_ak, v_st)]

    power = l_ab
    inv = [eye + l for l in l_ab]
    for _ in range(5):
        power = [_dot(_bf(x), _bf(stack(x))) for x in power]
        inv = [x + _dot(_bf(x), _bf(stack(pw))) for x, pw in zip(inv, power)]
    u = [_dot(_bf(x), _bf(stack(y))) for x, y in zip(inv, rhs)]
    uv_st = [jnp.concatenate([_bf(stack(x)), vst], axis=0) for x, vst in zip(u, v_st)]
    ys = [fs[t:] + _dot(m, x) for fs, m, x in zip(from_state, m_rbk, uv_st)]
    new_states = [st * jnp.exp(e) + _dot_tn(x, be)
                  for st, e, x, be in zip(states, cum_ends, uv_st, bk_end)]

    outs = []
    for p in pairs:
        d = ys[p] - head_sums(ys[p]) * (1.0 / HEAD_SIZE)
        var = head_sums(d * d) * (1.0 / HEAD_SIZE)
        yn = d * lax.rsqrt(var + GN_EPS)
        bonus = head_sums(rs[p] * ks[p] * r_k[p]) * vs[p]
        outs.append((yn * gn_g[p] + gn_b[p] + bonus) * gs[p])
    return outs, new_states


_PRM_NAMES = ("mu", "w0", "w_up", "a0", "a_up", "g_up", "k_k", "k_a", "pool_w", "pool_scale",
              "gn_g", "gn_b", "r_k")
_GATE_NAMES = ("v0", "v_down", "v_up")


def _timemix_kernel(*refs, has_gate):
    it = iter(refs)
    x_ref, w_ref, wout_ref = next(it), next(it), next(it)
    vf_ref = next(it) if has_gate else None
    names = _PRM_NAMES + (_GATE_NAMES if has_gate else ())
    prm = {nme: next(it) for nme in names}
    lng_ref, lnb_ref = next(it), next(it)
    xo_ref, xob_ref = next(it), next(it)
    v_ref = None if has_gate else next(it)
    z_ref, tail_ref, s_ref = next(it), next(it), next(it)

    t = CHUNK
    c = pl.program_id(0)

    @pl.when(c == 0)
    def _():
        tail_ref[...] = jnp.zeros_like(tail_ref)
        s_ref[...] = jnp.zeros_like(s_ref)

    slabs = [(s, b) for s in range(TM_CHUNKS) for b in range(BATCH)]

    def gather_rows(ref):
        return jnp.concatenate([ref[b, s * t:(s + 1) * t, :] for s, b in slabs], axis=0)

    z_ref[...] = _dot(_bf(gather_rows(x_ref)), w_ref[...])

    def pair_tile(x, b, p):
        return x[b * t:(b + 1) * t, p * LANE:(p + 1) * LANE]

    order = [(b, p) for b in range(BATCH) for p in range(N_PAIRS)]
    lanes = [slice(p * LANE, (p + 1) * LANE) for _, p in order]
    gn_g = [prm["gn_g"][:, sl] for sl in lanes]
    gn_b = [prm["gn_b"][:, sl] for sl in lanes]
    r_k = [prm["r_k"][:, sl] for sl in lanes]

    states = [s_ref[i] for i in range(len(order))]
    y_rwkv, y_pool, values = [], [], []
    for s in range(TM_CHUNKS):
        z_s = z_ref.at[s * TM_ROWS:(s + 1) * TM_ROWS]
        if s == 0:
            def history(b, r0, r1, lo, width):
                return tail_ref[b, r0:r1, lo:lo + width]
        else:
            def history(b, r0, r1, lo, width, base=(s - 1) * TM_ROWS):
                top = base + (b + 1) * t - MAX_WINDOW
                return z_ref[top + r0:top + r1, lo:lo + width]
        vf = None
        if has_gate:
            vf = jnp.concatenate([vf_ref[b, s * t:(s + 1) * t, :] for b in range(BATCH)], axis=0)
        r, k, v, an_w, bb, cum, g, pool = _prep_rows(z_s, history, c * TM_CHUNKS + s, vf, prm,
                                                     has_gate)
        tiles = [tuple(pair_tile(x, b, p) for x in (r, k, v, an_w, bb, cum, g)) for b, p in order]
        outs, states = _recurrence_chunk(tiles, states, gn_g, gn_b, r_k)
        y_rwkv += [jnp.concatenate([_bf(outs[b * N_PAIRS + p]) for p in range(N_PAIRS)], axis=1)
                   for b in range(BATCH)]
        y_pool.append(_bf(pool))
        values.append(v)

    last = (TM_CHUNKS - 1) * TM_ROWS
    for b in range(BATCH):
        tail_ref[b] = z_ref[last + (b + 1) * t - MAX_WINDOW:last + (b + 1) * t, :]

    mix = (_dot(jnp.concatenate(y_rwkv, axis=0), wout_ref[0:RWKV_WIDTH, :])
           + _dot(jnp.concatenate(y_pool, axis=0), wout_ref[RWKV_WIDTH:, :]))
    x1 = _layer_norm_rows(ALPHA * gather_rows(x_ref) + mix, lng_ref[...], lnb_ref[...])

    for i in range(len(order)):
        s_ref[i] = states[i]
    for j, (s, b) in enumerate(slabs):
        xo_ref[b, s * t:(s + 1) * t, :] = x1[j * t:(j + 1) * t, :]
        xob_ref[b, s * t:(s + 1) * t, :] = _bf(x1[j * t:(j + 1) * t, :])
        if not has_gate:
            v_ref[b, s * t:(s + 1) * t, :] = values[s][b * t:(b + 1) * t, :]


def _time_mix(x, w_in_bf, w_out_bf, vf, p, ln_g, ln_b, layer):
    has_gate = layer > 0
    t = CHUNK * TM_CHUNKS

    def rows(width):
        return pl.BlockSpec((BATCH, t, width), lambda c: (0, c, 0))

    def full(shape):
        return pl.BlockSpec(shape, lambda c: (0,) * len(shape))

    def layer_of(shape):
        return pl.BlockSpec((None,) + shape, lambda c: (layer,) + (0,) * len(shape),
                            pipeline_mode=pl.Buffered(1))

    in_specs = [rows(D_MODEL), layer_of((D_MODEL, C_USED)), layer_of((D_MODEL, D_MODEL))]
    args = [x, w_in_bf, w_out_bf]
    if has_gate:
        in_specs.append(rows(RWKV_WIDTH))
        args.append(vf)
    for nme in _PRM_NAMES + (_GATE_NAMES if has_gate else ()):
        in_specs.append(full(p[nme].shape))
        args.append(p[nme])
    in_specs += [layer_of((1, D_MODEL))] * 2
    args += [ln_g, ln_b]
    out_shape = [jax.ShapeDtypeStruct((BATCH, SEQ, D_MODEL), F32),
                 jax.ShapeDtypeStruct((BATCH, SEQ, D_MODEL), BF16)]
    out_specs = [rows(D_MODEL), rows(D_MODEL)]
    if not has_gate:
        out_shape.append(jax.ShapeDtypeStruct((BATCH, SEQ, RWKV_WIDTH), F32))
        out_specs.append(rows(RWKV_WIDTH))
    return pl.pallas_call(
        functools.partial(_timemix_kernel, has_gate=has_gate),
        grid=(SEQ // t,),
        in_specs=in_specs,
        out_specs=out_specs,
        out_shape=out_shape,
        scratch_shapes=[pltpu.VMEM((TM_CHUNKS * TM_ROWS, C_USED), F32),
                        pltpu.VMEM((BATCH, MAX_WINDOW, C_USED), F32),
                        pltpu.VMEM((BATCH * N_PAIRS, 2 * HEAD_SIZE, LANE), F32)],
        compiler_params=pltpu.CompilerParams(
            dimension_semantics=("arbitrary",), vmem_limit_bytes=VMEM_LIMIT),
        name="time_mix_gate" if has_gate else "time_mix",
    )(*args)


MLP_ROWS = 1024
MLP_FF = 512
MLP_STEPS = D_FF // MLP_FF
MLP_RES_ROWS = MLP_ROWS // MLP_STEPS


MLP_WEIGHT_BUFFERS = 3


def _mlp_kernel(xb_hbm, x_hbm, w1_hbm, w2_hbm, g_ref, b_ref, o_hbm, step_ref, *, layer, m):
    step_ref[0] = 0

    def step(xb_ref, xres_ref, w1_ref, w2_ref, xo_ref):
        f = step_ref[0]

        @pl.when(f == 0)
        def _():
            xo_ref[...] = jnp.zeros_like(xo_ref)

        h = jnp.square(jnp.maximum(_dot(xb_ref[...], _bf(w1_ref[...])), 0.0))
        xo_ref[...] += _dot(_bf(h), _bf(w2_ref[...]))
        rows = pl.ds(pl.multiple_of(f * MLP_RES_ROWS, MLP_RES_ROWS), MLP_RES_ROWS)
        xo_ref[rows, :] += ALPHA * xres_ref[...]

        @pl.when(f == MLP_STEPS - 1)
        def _():
            xo_ref[...] = _layer_norm_rows(xo_ref[...], g_ref[layer], b_ref[layer])

        step_ref[0] = jnp.where(f == MLP_STEPS - 1, 0, f + 1)

    deep = pl.Buffered(MLP_WEIGHT_BUFFERS)
    row_spec = pl.BlockSpec((MLP_ROWS, D_MODEL), lambda i, f: (i, 0))
    pltpu.emit_pipeline(
        step,
        grid=(m // MLP_ROWS, MLP_STEPS),
        in_specs=[row_spec,
                  pl.BlockSpec((MLP_RES_ROWS, D_MODEL), lambda i, f: (i * MLP_STEPS + f, 0)),
                  pl.BlockSpec((None, D_MODEL, MLP_FF), lambda i, f: (layer, 0, f),
                               pipeline_mode=deep),
                  pl.BlockSpec((None, MLP_FF, D_MODEL), lambda i, f: (layer, f, 0),
                               pipeline_mode=deep)],
        out_specs=[row_spec],
    )(xb_hbm, x_hbm, w1_hbm, w2_hbm, o_hbm)


def _mlp(x_bf, x, w1, w2, g, b, layer):
    m = x.shape[0]
    vec_spec = pl.BlockSpec(memory_space=pltpu.VMEM)
    hbm_spec = pl.BlockSpec(memory_space=pl.ANY)
    return pl.pallas_call(
        functools.partial(_mlp_kernel, layer=layer, m=m),
        in_specs=[hbm_spec, hbm_spec, hbm_spec, hbm_spec, vec_spec, vec_spec],
        out_specs=hbm_spec,
        out_shape=jax.ShapeDtypeStruct((m, D_MODEL), F32),
        scratch_shapes=[pltpu.SMEM((1,), jnp.int32)],
        compiler_params=pltpu.CompilerParams(vmem_limit_bytes=VMEM_LIMIT),
        name="mlp_ln",
    )(x_bf, x, w1, w2, g, b)


W_PREP_ROWS = 256


def _w_in_prep_kernel(wt_ref, o_ref):
    gap = jnp.zeros((C_POOL - C_SHIFT, W_PREP_ROWS), F32)
    padded = jnp.concatenate([wt_ref[0:C_SHIFT, :], gap, wt_ref[C_SHIFT:C_IN, :]], axis=0)
    o_ref[...] = _bf(padded.T)


def _w_in_prep(w_in):
    w_t = jnp.swapaxes(w_in, 1, 2)
    return pl.pallas_call(
        _w_in_prep_kernel,
        grid=(DEPTH, D_MODEL // W_PREP_ROWS),
        in_specs=[pl.BlockSpec((None, C_IN, W_PREP_ROWS), lambda l, i: (l, 0, i))],
        out_specs=pl.BlockSpec((None, W_PREP_ROWS, C_USED), lambda l, i: (l, i, 0)),
        out_shape=jax.ShapeDtypeStruct((DEPTH, D_MODEL, C_USED), BF16),
        compiler_params=pltpu.CompilerParams(
            dimension_semantics=("arbitrary", "arbitrary"), vmem_limit_bytes=VMEM_LIMIT),
        name="w_in_prep",
    )(w_t)


def _pad_rows(w, rows, at=0):
    out = jnp.zeros((rows, w.shape[1]), w.dtype)
    return out.at[at:at + w.shape[0]].set(w)


def _layer_params(l, mu, w0, w_up, a0, a_up, g_up, v0, v_down, v_up, k_k, k_a, r_k,
                  gn_g, gn_b, pool_w, pool_scale):
    p = {
        "mu": jnp.concatenate([mu[l], jnp.zeros((C_USED - C_SHIFT,), F32)])[None, :],
        "w0": w0[l][None, :],
        "w_up": _bf(_pad_rows(w_up[l], LANE, 0)),
        "a0": a0[l][None, :],
        "a_up": _bf(_pad_rows(a_up[l], LANE, R_DECAY)),
        "g_up": _bf(_pad_rows(g_up[l], 2 * LANE, 0)),
        "k_k": k_k[l][None, :],
        "k_a": k_a[l][None, :],
        "pool_w": _bf(pool_w[l]),
        "pool_scale": pool_scale[l][None, :],
        "r_k": r_k[l].reshape(1, RWKV_WIDTH),
        "gn_g": gn_g[l][None, :],
        "gn_b": gn_b[l][None, :],
    }
    if l > 0:
        p["v0"] = v0[l - 1][None, :]
        p["v_down"] = _bf(jnp.pad(v_down[l - 1], ((0, 0), (0, LANE - R_MV))))
        p["v_up"] = _bf(_pad_rows(v_up[l - 1], LANE, 0))
    return p


def kernel(x, w_in, mu, w0, w_up, a0, a_up, g_up, v0, v_down, v_up, k_k, k_a, r_k, gn_g, gn_b,
           pool_w, pool_scale, w_out, ln1_g, ln1_b, mlp_w1, mlp_w2, ln2_g, ln2_b):
    m = BATCH * SEQ
    w_in_bf = _w_in_prep(w_in)
    w_out_bf = _bf(w_out)
    ln1_g, ln1_b, ln2_g, ln2_b = (v.reshape(DEPTH, 1, D_MODEL) for v in (ln1_g, ln1_b, ln2_g, ln2_b))
    v_first = None
    for l in range(DEPTH):
        p = _layer_params(l, mu, w0, w_up, a0, a_up, g_up, v0, v_down, v_up, k_k, k_a,
                          r_k, gn_g, gn_b, pool_w, pool_scale)
        res = _time_mix(x, w_in_bf, w_out_bf, v_first, p, ln1_g, ln1_b, l)
        x, x_bf = res[0], res[1]
        if l == 0:
            v_first = res[2]
        x = _mlp(x_bf.reshape(m, D_MODEL), x.reshape(m, D_MODEL), mlp_w1, mlp_w2,
                 ln2_g, ln2_b, l).reshape(BATCH, SEQ, D_MODEL)
    return x
```

```python
import functools

import jax
import jax.numpy as jnp
from jax import lax
from jax.experimental import pallas as pl
from jax.experimental.pallas import tpu as pltpu

D_MODEL = 2048
BATCH = 2
SEQ = 8192
DEPTH = 2
RWKV_WIDTH = 1024
POOL_WIDTH = 1024
HEAD_SIZE = 64
POOL_WINDOWS = (2, 4, 8, 16)
POOL_GROUP = 256
R_DECAY = 64
R_AAA = 64
R_MV = 32
R_GATE = 160
C_SHIFT = 3 * RWKV_WIDTH + R_DECAY + R_AAA + R_GATE
C_IN = C_SHIFT + POOL_WIDTH
D_FF = 4 * D_MODEL
ALPHA = (2.0 * DEPTH) ** 0.25
LN_EPS = 1e-5
GN_EPS = 64e-5
L2_EPS = 1e-12

LANE = 128
CHUNK = 64
N_PAIRS = RWKV_WIDTH // LANE
MAX_WINDOW = max(POOL_WINDOWS)

C_LR = 3 * RWKV_WIDTH
C_LR_W = 3 * LANE
C_POOL = C_LR + C_LR_W
C_USED = C_POOL + POOL_WIDTH

VMEM_LIMIT = 60 * 1024 * 1024

F32 = jnp.float32
BF16 = jnp.bfloat16
F32_PIECES = 3


def _bf(x):
    return x.astype(BF16)


def _dot(a, b):
    return jnp.dot(a, b, preferred_element_type=F32)


def _dot_nt(a, b):
    return lax.dot_general(a, b, (((1,), (1,)), ((), ())), preferred_element_type=F32)


def _dot_tn(a, b):
    return lax.dot_general(a, b, (((0,), (0,)), ((), ())), preferred_element_type=F32)


def _split_dot(x, w_bf, n_terms):
    acc = None
    rem = x
    for _ in range(n_terms):
        piece = _bf(rem)
        term = _dot(piece, w_bf)
        acc = term if acc is None else acc + term
        rem = rem - piece.astype(F32)
    return acc


def _split_dot_left(w_bf, x, n_terms):
    acc = None
    rem = x
    for _ in range(n_terms):
        piece = _bf(rem)
        term = _dot(w_bf, piece)
        acc = term if acc is None else acc + term
        rem = rem - piece.astype(F32)
    return acc


def _sigmoid(x):
    return 1.0 / (1.0 + jnp.exp(-x))


def _layer_norm_rows(h, g, b):
    mean = jnp.mean(h, axis=-1, keepdims=True)
    d = h - mean
    var = jnp.mean(d * d, axis=-1, keepdims=True)
    return d * lax.rsqrt(var + LN_EPS) * g + b


TM_ROWS = BATCH * CHUNK
TM_CHUNKS = 2


def _prep_rows(z_ref, history, chunk, vf, prm, has_gate):
    t = CHUNK

    def token_shift_mix(lo, width):
        cur = z_ref[:, lo:lo + width]
        row = lax.broadcasted_iota(jnp.int32, cur.shape, 0)
        prev = pltpu.roll(cur, 1, axis=0)
        for b in range(BATCH):
            last = history(b, MAX_WINDOW - 1, MAX_WINDOW, lo, width)
            prev = jnp.where(row == b * t, last, prev)
        return cur + (prev - cur) * prm["mu"][:, lo:lo + width]

    zl = token_shift_mix(C_LR, C_LR_W)
    xwa = zl[:, 0:LANE]
    xg = zl[:, LANE:C_LR_W]
    w_pre = prm["w0"][...] + _dot(_bf(jnp.tanh(xwa)), prm["w_up"][...])
    lw = -jnp.exp(-0.5) * _sigmoid(w_pre)
    a = _sigmoid(prm["a0"][...] + _dot(_bf(xwa), prm["a_up"][...]))
    g = _dot(_bf(_sigmoid(xg)), prm["g_up"][...])

    rr = lax.broadcasted_iota(jnp.int32, (TM_ROWS, TM_ROWS), 0)
    cc = lax.broadcasted_iota(jnp.int32, (TM_ROWS, TM_ROWS), 1)
    chunk_bits = t.bit_length() - 1
    same_chunk = jnp.right_shift(rr, chunk_bits) == jnp.right_shift(cc, chunk_bits)
    tri = jnp.where((rr >= cc) & same_chunk, 1.0, 0.0).astype(BF16)
    cum = _split_dot_left(tri, lw, F32_PIECES)

    r = token_shift_mix(0, RWKV_WIDTH)
    k = token_shift_mix(RWKV_WIDTH, RWKV_WIDTH)
    v = token_shift_mix(2 * RWKV_WIDTH, RWKV_WIDTH)
    if has_gate:
        low = _dot(_bf(v), prm["v_down"][...])
        gate = _sigmoid(prm["v0"][...] + _dot(_bf(low), prm["v_up"][...]))
        v = v + (vf - v) * gate

    kk = k * prm["k_k"][...]
    sr = lax.broadcasted_iota(jnp.int32, (2 * LANE, 2 * LANE), 0)
    sc = lax.broadcasted_iota(jnp.int32, (2 * LANE, 2 * LANE), 1)
    head_bits = HEAD_SIZE.bit_length() - 1
    seg = jnp.where(jnp.right_shift(sr, head_bits) == jnp.right_shift(sc, head_bits),
                    1.0, 0.0).astype(BF16)
    sq = kk * kk
    ss = jnp.concatenate(
        [_split_dot(sq[:, j:j + 2 * LANE], seg, F32_PIECES)
         for j in range(0, RWKV_WIDTH, 2 * LANE)], axis=1)
    kkn = kk / jnp.maximum(jnp.sqrt(ss), L2_EPS)
    an_w = -kkn * jnp.exp(-lw)
    bb = kkn * a
    k_mod = k * (1.0 + (a - 1.0) * prm["k_a"][...])

    pos = chunk * t + lax.broadcasted_iota(jnp.int32, (t, POOL_GROUP), 0) + 1
    outs = []
    for gi, win in enumerate(POOL_WINDOWS):
        lo = C_POOL + gi * POOL_GROUP
        cnt = jnp.minimum(pos, win).astype(F32)
        d_rows = []
        for b in range(BATCH):
            u = z_ref[b * t:(b + 1) * t, lo:lo + POOL_GROUP]
            e = jnp.concatenate([history(b, 0, MAX_WINDOW, lo, POOL_GROUP), u], axis=0)
            step = 1
            while step < win:
                e = e + pltpu.roll(e, step, axis=0)
                step *= 2
            d_rows.append(e[MAX_WINDOW:MAX_WINDOW + t, :] / cnt - u)
        outs.append(_dot(_bf(jnp.concatenate(d_rows, axis=0)), prm["pool_w"][gi]))
    y_pool = jnp.concatenate(outs, axis=1) * prm["pool_scale"][...]
    return r, k_mod, v, an_w, bb, cum, g, y_pool


def _recurrence_chunk(tiles, states, gn_g, gn_b, r_k):
    t = CHUNK
    lane = lax.broadcasted_iota(jnp.int32, (t, LANE), 1)
    row = lax.broadcasted_iota(jnp.int32, (t, LANE), 0)
    head0 = lane < HEAD_SIZE
    col = jnp.bitwise_and(lane, t - 1)
    strict = row > col
    incl = row >= col
    eye = jnp.where(row == col, 1.0, 0.0)

    def stack(x):
        return jnp.concatenate([jnp.where(head0, x, 0.0), jnp.where(head0, 0.0, x)], axis=0)

    def head_sums(x):
        s0 = jnp.sum(jnp.where(head0, x, 0.0), axis=-1, keepdims=True)
        s1 = jnp.sum(jnp.where(head0, 0.0, x), axis=-1, keepdims=True)
        return jnp.where(head0, s0, s1)

    rs, ks, vs, ans, bbs, cums, gs = (list(x) for x in zip(*tiles))
    pairs = range(len(tiles))

    cum_ends = [c[t - 1:t, :] for c in cums]
    decay_in = [jnp.exp(c) for c in cums]
    decay_out = [jnp.exp(-c) for c in cums]
    decay_rest = [jnp.exp(e - c) for c, e in zip(cums, cum_ends)]
    v_st = [_bf(stack(v)) for v in vs]
    ar = [jnp.concatenate([_bf(a * d), _bf(r * d)], axis=0)
          for a, r, d in zip(ans, rs, decay_in)]
    bk_st = [jnp.concatenate([_bf(stack(b * d)), _bf(stack(k * d))], axis=0)
             for b, k, d in zip(bbs, ks, decay_out)]
    bk_end = [jnp.concatenate([_bf(stack(b * d)), _bf(stack(k * d))], axis=0)
              for b, k, d in zip(bbs, ks, decay_rest)]

    scores = [_dot_nt(x, y) for x, y in zip(ar, bk_st)]
    l_ab = [jnp.where(strict, sc[:t, :LANE], 0.0) for sc in scores]
    l_ak = [jnp.where(strict, sc[:t, LANE:], 0.0) for sc in scores]
    m_rbk = [jnp.concatenate([_bf(jnp.where(incl, sc[t:, :LANE], 0.0)),
                              _bf(jnp.where(incl, sc[t:, LANE:], 0.0))], axis=1) for sc in scores]
    from_state = [_dot_nt(x, _bf(st)) for x, st in zip(ar, states)]
    rhs = [fs[:t] + _dot(_bf(l), vst) for fs, l, vst in zip(from_state, l_ak, v_st)]

    power = l_ab
    inv = [eye + l for l in l_ab]
    for _ in range(5):
        power = [_dot(_bf(x), _bf(stack(x))) for x in power]
        inv = [x + _dot(_bf(x), _bf(stack(pw))) for x, pw in zip(inv, power)]
    u = [_dot(_bf(x), _bf(stack(y))) for x, y in zip(inv, rhs)]
    uv_st = [jnp.concatenate([_bf(stack(x)), vst], axis=0) for x, vst in zip(u, v_st)]
    ys = [fs[t:] + _dot(m, x) for fs, m, x in zip(from_state, m_rbk, uv_st)]
    new_states = [st * jnp.exp(e) + _dot_tn(x, be)
                  for st, e, x, be in zip(states, cum_ends, uv_st, bk_end)]

    outs = []
    for p in pairs:
        d = ys[p] - head_sums(ys[p]) * (1.0 / HEAD_SIZE)
        var = head_sums(d * d) * (1.0 / HEAD_SIZE)
        yn = d * lax.rsqrt(var + GN_EPS)
        bonus = head_sums(rs[p] * ks[p] * r_k[p]) * vs[p]
        outs.append((yn * gn_g[p] + gn_b[p] + bonus) * gs[p])
    return outs, new_states


_PRM_NAMES = ("mu", "w0", "w_up", "a0", "a_up", "g_up", "k_k", "k_a", "pool_w", "pool_scale",
              "gn_g", "gn_b", "r_k")
_GATE_NAMES = ("v0", "v_down", "v_up")


def _timemix_kernel(*refs, has_gate):
    it = iter(refs)
    x_ref, w_ref, wout_ref = next(it), next(it), next(it)
    vf_ref = next(it) if has_gate else None
    names = _PRM_NAMES + (_GATE_NAMES if has_gate else ())
    prm = {nme: next(it) for nme in names}
    lng_ref, lnb_ref = next(it), next(it)
    xo_ref, xob_ref = next(it), next(it)
    v_ref = None if has_gate else next(it)
    z_ref, tail_ref, s_ref = next(it), next(it), next(it)

    t = CHUNK
    c = pl.program_id(0)

    @pl.when(c == 0)
    def _():
        tail_ref[...] = jnp.zeros_like(tail_ref)
        s_ref[...] = jnp.zeros_like(s_ref)

    slabs = [(s, b) for s in range(TM_CHUNKS) for b in range(BATCH)]

    def gather_rows(ref):
        return jnp.concatenate([ref[b, s * t:(s + 1) * t, :] for s, b in slabs], axis=0)

    z_ref[...] = _dot(_bf(gather_rows(x_ref)), w_ref[...])

    def pair_tile(x, b, p):
        return x[b * t:(b + 1) * t, p * LANE:(p + 1) * LANE]

    order = [(b, p) for b in range(BATCH) for p in range(N_PAIRS)]
    lanes = [slice(p * LANE, (p + 1) * LANE) for _, p in order]
    gn_g = [prm["gn_g"][:, sl] for sl in lanes]
    gn_b = [prm["gn_b"][:, sl] for sl in lanes]
    r_k = [prm["r_k"][:, sl] for sl in lanes]

    states = [s_ref[i] for i in range(len(order))]
    y_rwkv, y_pool, values = [], [], []
    for s in range(TM_CHUNKS):
        z_s = z_ref.at[s * TM_ROWS:(s + 1) * TM_ROWS]
        if s == 0:
            def history(b, r0, r1, lo, width):
                return tail_ref[b, r0:r1, lo:lo + width]
        else:
            def history(b, r0, r1, lo, width, base=(s - 1) * TM_ROWS):
                top = base + (b + 1) * t - MAX_WINDOW
                return z_ref[top + r0:top + r1, lo:lo + width]
        vf = None
        if has_gate:
            vf = jnp.concatenate([vf_ref[b, s * t:(s + 1) * t, :] for b in range(BATCH)], axis=0)
        r, k, v, an_w, bb, cum, g, pool = _prep_rows(z_s, history, c * TM_CHUNKS + s, vf, prm,
                                                     has_gate)
        tiles = [tuple(pair_tile(x, b, p) for x in (r, k, v, an_w, bb, cum, g)) for b, p in order]
        outs, states = _recurrence_chunk(tiles, states, gn_g, gn_b, r_k)
        y_rwkv += [jnp.concatenate([_bf(outs[b * N_PAIRS + p]) for p in range(N_PAIRS)], axis=1)
                   for b in range(BATCH)]
        y_pool.append(_bf(pool))
        values.append(v)

    last = (TM_CHUNKS - 1) * TM_ROWS
    for b in range(BATCH):
        tail_ref[b] = z_ref[last + (b + 1) * t - MAX_WINDOW:last + (b + 1) * t, :]

    mix = (_dot(jnp.concatenate(y_rwkv, axis=0), wout_ref[0:RWKV_WIDTH, :])
           + _dot(jnp.concatenate(y_pool, axis=0), wout_ref[RWKV_WIDTH:, :]))
    x1 = _layer_norm_rows(ALPHA * gather_rows(x_ref) + mix, lng_ref[...], lnb_ref[...])

    for i in range(len(order)):
        s_ref[i] = states[i]
    for j, (s, b) in enumerate(slabs):
        xo_ref[b, s * t:(s + 1) * t, :] = x1[j * t:(j + 1) * t, :]
        xob_ref[b, s * t:(s + 1) * t, :] = _bf(x1[j * t:(j + 1) * t, :])
        if not has_gate:
            v_ref[b, s * t:(s + 1) * t, :] = values[s][b * t:(b + 1) * t, :]


def _time_mix(x, w_in_bf, w_out_bf, vf, p, ln_g, ln_b, layer):
    has_gate = layer > 0
    t = CHUNK * TM_CHUNKS

    def rows(width):
        return pl.BlockSpec((BATCH, t, width), lambda c: (0, c, 0))

    def full(shape):
        return pl.BlockSpec(shape, lambda c: (0,) * len(shape))

    def layer_of(shape):
        return pl.BlockSpec((None,) + shape, lambda c: (layer,) + (0,) * len(shape),
                            pipeline_mode=pl.Buffered(1))

    in_specs = [rows(D_MODEL), layer_of((D_MODEL, C_USED)), layer_of((D_MODEL, D_MODEL))]
    args = [x, w_in_bf, w_out_bf]
    if has_gate:
        in_specs.append(rows(RWKV_WIDTH))
        args.append(vf)
    for nme in _PRM_NAMES + (_GATE_NAMES if has_gate else ()):
        in_specs.append(full(p[nme].shape))
        args.append(p[nme])
    in_specs += [layer_of((1, D_MODEL))] * 2
    args += [ln_g, ln_b]
    out_shape = [jax.ShapeDtypeStruct((BATCH, SEQ, D_MODEL), F32),
                 jax.ShapeDtypeStruct((BATCH, SEQ, D_MODEL), BF16)]
    out_specs = [rows(D_MODEL), rows(D_MODEL)]
    if not has_gate:
        out_shape.append(jax.ShapeDtypeStruct((BATCH, SEQ, RWKV_WIDTH), F32))
        out_specs.append(rows(RWKV_WIDTH))
    return pl.pallas_call(
        functools.partial(_timemix_kernel, has_gate=has_gate),
        grid=(SEQ // t,),
        in_specs=in_specs,
        out_specs=out_specs,
        out_shape=out_shape,
        scratch_shapes=[pltpu.VMEM((TM_CHUNKS * TM_ROWS, C_USED), F32),
                        pltpu.VMEM((BATCH, MAX_WINDOW, C_USED), F32),
                        pltpu.VMEM((BATCH * N_PAIRS, 2 * HEAD_SIZE, LANE), F32)],
        compiler_params=pltpu.CompilerParams(
            dimension_semantics=("arbitrary",), vmem_limit_bytes=VMEM_LIMIT),
        name="time_mix_gate" if has_gate else "time_mix",
    )(*args)


MLP_ROWS = 1024
MLP_FF = 512
MLP_STEPS = D_FF // MLP_FF
MLP_RES_ROWS = MLP_ROWS // MLP_STEPS


MLP_WEIGHT_BUFFERS = 3


def _mlp_kernel(xb_hbm, x_hbm, w1_hbm, w2_hbm, g_ref, b_ref, o_hbm, step_ref, *, layer, m):
    step_ref[0] = 0

    def step(xb_ref, xres_ref, w1_ref, w2_ref, xo_ref):
        f = step_ref[0]

        @pl.when(f == 0)
        def _():
            xo_ref[...] = jnp.zeros_like(xo_ref)

        h = jnp.square(jnp.maximum(_dot(xb_ref[...], _bf(w1_ref[...])), 0.0))
        xo_ref[...] += _dot(_bf(h), _bf(w2_ref[...]))
        rows = pl.ds(pl.multiple_of(f * MLP_RES_ROWS, MLP_RES_ROWS), MLP_RES_ROWS)
        xo_ref[rows, :] += ALPHA * xres_ref[...]

        @pl.when(f == MLP_STEPS - 1)
        def _():
            xo_ref[...] = _layer_norm_rows(xo_ref[...], g_ref[layer], b_ref[layer])

        step_ref[0] = jnp.where(f == MLP_STEPS - 1, 0, f + 1)

    deep = pl.Buffered(MLP_WEIGHT_BUFFERS)
    row_spec = pl.BlockSpec((MLP_ROWS, D_MODEL), lambda i, f: (i, 0))
    pltpu.emit_pipeline(
        step,
        grid=(m // MLP_ROWS, MLP_STEPS),
        in_specs=[pl.BlockSpec((MLP_ROWS, D_MODEL), lambda i, f: (i, 0), pipeline_mode=deep),
                  pl.BlockSpec((MLP_RES_ROWS, D_MODEL), lambda i, f: (i * MLP_STEPS + f, 0)),
                  pl.BlockSpec((None, D_MODEL, MLP_FF), lambda i, f: (layer, 0, f),
                               pipeline_mode=deep),
                  pl.BlockSpec((None, MLP_FF, D_MODEL), lambda i, f: (layer, f, 0),
                               pipeline_mode=deep)],
        out_specs=[row_spec],
    )(xb_hbm, x_hbm, w1_hbm, w2_hbm, o_hbm)


def _mlp(x_bf, x, w1, w2, g, b, layer):
    m = x.shape[0]
    vec_spec = pl.BlockSpec(memory_space=pltpu.VMEM)
    hbm_spec = pl.BlockSpec(memory_space=pl.ANY)
    return pl.pallas_call(
        functools.partial(_mlp_kernel, layer=layer, m=m),
        in_specs=[hbm_spec, hbm_spec, hbm_spec, hbm_spec, vec_spec, vec_spec],
        out_specs=hbm_spec,
        out_shape=jax.ShapeDtypeStruct((m, D_MODEL), F32),
        scratch_shapes=[pltpu.SMEM((1,), jnp.int32)],
        compiler_params=pltpu.CompilerParams(vmem_limit_bytes=VMEM_LIMIT),
        name="mlp_ln",
    )(x_bf, x, w1, w2, g, b)


W_PREP_ROWS = 256


def _w_in_prep_kernel(wt_ref, o_ref):
    gap = jnp.zeros((C_POOL - C_SHIFT, W_PREP_ROWS), F32)
    padded = jnp.concatenate([wt_ref[0:C_SHIFT, :], gap, wt_ref[C_SHIFT:C_IN, :]], axis=0)
    o_ref[...] = _bf(padded.T)


def _w_in_prep(w_in):
    w_t = jnp.swapaxes(w_in, 1, 2)
    return pl.pallas_call(
        _w_in_prep_kernel,
        grid=(DEPTH, D_MODEL // W_PREP_ROWS),
        in_specs=[pl.BlockSpec((None, C_IN, W_PREP_ROWS), lambda l, i: (l, 0, i))],
        out_specs=pl.BlockSpec((None, W_PREP_ROWS, C_USED), lambda l, i: (l, i, 0)),
        out_shape=jax.ShapeDtypeStruct((DEPTH, D_MODEL, C_USED), BF16),
        compiler_params=pltpu.CompilerParams(
            dimension_semantics=("arbitrary", "arbitrary"), vmem_limit_bytes=VMEM_LIMIT),
        name="w_in_prep",
    )(w_t)


def _pad_rows(w, rows, at=0):
    out = jnp.zeros((rows, w.shape[1]), w.dtype)
    return out.at[at:at + w.shape[0]].set(w)


def _layer_params(l, mu, w0, w_up, a0, a_up, g_up, v0, v_down, v_up, k_k, k_a, r_k,
                  gn_g, gn_b, pool_w, pool_scale):
    p = {
        "mu": jnp.concatenate([mu[l], jnp.zeros((C_USED - C_SHIFT,), F32)])[None, :],
        "w0": w0[l][None, :],
        "w_up": _bf(_pad_rows(w_up[l], LANE, 0)),
        "a0": a0[l][None, :],
        "a_up": _bf(_pad_rows(a_up[l], LANE, R_DECAY)),
        "g_up": _bf(_pad_rows(g_up[l], 2 * LANE, 0)),
        "k_k": k_k[l][None, :],
        "k_a": k_a[l][None, :],
        "pool_w": _bf(pool_w[l]),
        "pool_scale": pool_scale[l][None, :],
        "r_k": r_k[l].reshape(1, RWKV_WIDTH),
        "gn_g": gn_g[l][None, :],
        "gn_b": gn_b[l][None, :],
    }
    if l > 0:
        p["v0"] = v0[l - 1][None, :]
        p["v_down"] = _bf(jnp.pad(v_down[l - 1], ((0, 0), (0, LANE - R_MV))))
        p["v_up"] = _bf(_pad_rows(v_up[l - 1], LANE, 0))
    return p


def kernel(x, w_in, mu, w0, w_up, a0, a_up, g_up, v0, v_down, v_up, k_k, k_a, r_k, gn_g, gn_b,
           pool_w, pool_scale, w_out, ln1_g, ln1_b, mlp_w1, mlp_w2, ln2_g, ln2_b):
    m = BATCH * SEQ
    w_in_bf = _w_in_prep(w_in)
    w_out_bf = _bf(w_out)
    ln1_g, ln1_b, ln2_g, ln2_b = (v.reshape(DEPTH, 1, D_MODEL) for v in (ln1_g, ln1_b, ln2_g, ln2_b))
    v_first = None
    for l in range(DEPTH):
        p = _layer_params(l, mu, w0, w_up, a0, a_up, g_up, v0, v_down, v_up, k_k, k_a,
                          r_k, gn_g, gn_b, pool_w, pool_scale)
        res = _time_mix(x, w_in_bf, w_out_bf, v_first, p, ln1_g, ln1_b, l)
        x, x_bf = res[0], res[1]
        if l == 0:
            v_first = res[2]
        x = _mlp(x_bf.reshape(m, D_MODEL), x.reshape(m, D_MODEL), mlp_w1, mlp_w2,
                 ln2_g, ln2_b, l).reshape(BATCH, SEQ, D_MODEL)
    return x
```
